```python
import math
import jax, jax.numpy as jnp
from jax import lax
import numpy as np

D_MODEL = 1024
BATCH = 8
SEQ = 4096
DEPTH = 4

HEAD_DIM = 64
D_ATTN = D_MODEL // 2
ATTN_HEADS = D_ATTN // HEAD_DIM
Q_BLOCK = 128
D_CONV = D_MODEL // 4
CONV_K = 3
D_POOL = D_MODEL // 4
POOL_WINDOWS = (2, 4, 8, 16)
POOL_GROUPS = len(POOL_WINDOWS)
POOL_GROUP_DIM = D_POOL // POOL_GROUPS
POOL_OUT_DIM = D_MODEL // POOL_GROUPS
N_BRANCHES = 3
D_FF = -(-8 * D_MODEL // (3 * 256)) * 256
EPS = 1e-6

IN_SIZES = (D_ATTN, D_ATTN, D_ATTN, ATTN_HEADS, D_CONV, D_CONV, D_CONV, D_POOL, N_BRANCHES * D_MODEL)
D_IN = sum(IN_SIZES)
IN_SPLITS = tuple(int(v) for v in np.cumsum(IN_SIZES)[:-1])

kernel_name = "fox_conv_pool_gated_hybrid"


def rmsnorm(x, g):
    xf = x.astype(jnp.float32)
    y = xf * lax.rsqrt(jnp.mean(xf * xf, axis=-1, keepdims=True) + EPS)
    return (y * g.astype(jnp.float32)).astype(x.dtype)


def forgetting_attention(q, k, v, logf):
    S = q.shape[2]
    c = jnp.cumsum(logf, axis=-1)
    scale = HEAD_DIM ** -0.5
    outs = []
    for i in range(S // Q_BLOCK):
        q0, q1 = i * Q_BLOCK, (i + 1) * Q_BLOCK
        qb = q[:, :, q0:q1]
        kb = k[:, :, :q1]
        vb = v[:, :, :q1]
        logits = jnp.einsum('bhqd,bhkd->bhqk', qb, kb, preferred_element_type=jnp.float32) * scale
        logits = logits + (c[:, :, q0:q1, None] - c[:, :, None, :q1])
        causal = (q0 + jnp.arange(Q_BLOCK))[:, None] >= jnp.arange(q1)[None, :]
        logits = jnp.where(causal, logits, -jnp.inf)
        p = jax.nn.softmax(logits, axis=-1)
        outs.append(jnp.einsum('bhqk,bhkd->bhqd', p.astype(vb.dtype), vb))
    return jnp.concatenate(outs, axis=2)


def short_conv_mixer(u, b_gate, c_gate, w):
    S = u.shape[1]
    z = c_gate * u
    zp = jnp.pad(z, ((0, 0), (CONV_K - 1, 0), (0, 0)))
    conv = w[0] * zp[:, 0:S]
    for j in range(1, CONV_K):
        conv = conv + w[j] * zp[:, j:j + S]
    return b_gate * conv


def pooling_mixer(u, w_grp, scale):
    Bsz, S, _ = u.shape
    uf = u.astype(jnp.float32)
    cs = jnp.cumsum(uf, axis=1)
    t = jnp.arange(S, dtype=jnp.float32)
    groups = []
    for g, w in enumerate(POOL_WINDOWS):
        sl = slice(g * POOL_GROUP_DIM, (g + 1) * POOL_GROUP_DIM)
        csg = cs[:, :, sl]
        lagged = jnp.pad(csg[:, :S - w], ((0, 0), (w, 0), (0, 0)))
        counts = jnp.minimum(t + 1.0, float(w))[None, :, None]
        groups.append((csg - lagged) / counts - uf[:, :, sl])
    d = jnp.stack(groups, axis=2).astype(u.dtype)
    out = jnp.einsum('bsgc,gcd->bsgd', d, w_grp).reshape(Bsz, S, D_MODEL)
    return out * scale


def setup_inputs(seed: int = 0) -> dict:
    key = jax.random.key(seed)
    ks = jax.random.split(key, 16)
    nrm = lambda k, shape, fan_in: jax.random.normal(k, shape, jnp.float32) * (fan_in ** -0.5)
    return {
        "x": jax.random.normal(ks[0], (BATCH, SEQ, D_MODEL), jnp.float32),
        "norm_mix_g": 1.0 + 0.02 * jax.random.normal(ks[1], (DEPTH, D_MODEL), jnp.float32),
        "w_in": nrm(ks[2], (DEPTH, D_MODEL, D_IN), D_MODEL),
        "forget_b": jax.random.uniform(ks[3], (DEPTH, ATTN_HEADS), jnp.float32, 2.0, 5.0),
        "q_norm_g": 1.0 + 0.02 * jax.random.normal(ks[4], (DEPTH, HEAD_DIM), jnp.float32),
        "k_norm_g": 1.0 + 0.02 * jax.random.normal(ks[5], (DEPTH, HEAD_DIM), jnp.float32),
        "w_attn_out": nrm(ks[6], (DEPTH, D_ATTN, D_MODEL), D_ATTN),
        "conv_w": nrm(ks[7], (DEPTH, CONV_K, D_CONV), CONV_K),
        "w_conv_out": nrm(ks[8], (DEPTH, D_CONV, D_MODEL), D_CONV),
        "pool_w": nrm(ks[9], (DEPTH, POOL_GROUPS, POOL_GROUP_DIM, POOL_OUT_DIM), POOL_GROUP_DIM),
        "pool_scale": 1.0 + 0.1 * jax.random.normal(ks[10], (DEPTH, D_MODEL), jnp.float32),
        "w_o": nrm(ks[11], (DEPTH, D_MODEL, D_MODEL), D_MODEL),
        "norm_ffn_g": 1.0 + 0.02 * jax.random.normal(ks[12], (DEPTH, D_MODEL), jnp.float32),
        "w_ffn_in": nrm(ks[13], (DEPTH, D_MODEL, 2 * D_FF), D_MODEL),
        "w_ffn_out": nrm(ks[14], (DEPTH, D_FF, D_MODEL), D_FF),
    }


def reference(x, norm_mix_g, w_in, forget_b, q_norm_g, k_norm_g, w_attn_out, conv_w,
              w_conv_out, pool_w, pool_scale, w_o, norm_ffn_g, w_ffn_in, w_ffn_out):
    Bsz, S, _ = x.shape

    def heads(t):
        return t.reshape(Bsz, S, ATTN_HEADS, HEAD_DIM).transpose(0, 2, 1, 3)

    for l in range(DEPTH):
        h = rmsnorm(x, norm_mix_g[l])
        proj = h @ w_in[l]
        q, k, v, f_logit, cx, cb, cc, px, gate_logit = jnp.split(proj, IN_SPLITS, axis=-1)

        qh = rmsnorm(heads(q), q_norm_g[l])
        kh = rmsnorm(heads(k), k_norm_g[l])
        vh = heads(v)
        logf = jax.nn.log_sigmoid((f_logit + forget_b[l]).astype(jnp.float32)).transpose(0, 2, 1)
        a = forgetting_attention(qh, kh, vh, logf).transpose(0, 2, 1, 3).reshape(Bsz, S, D_ATTN)
        y_attn = a @ w_attn_out[l]

        y_conv = short_conv_mixer(cx, cb, cc, conv_w[l]) @ w_conv_out[l]

        y_pool = pooling_mixer(px, pool_w[l], pool_scale[l])

        g = jax.nn.sigmoid(gate_logit).reshape(Bsz, S, N_BRANCHES, D_MODEL)
        merged = g[:, :, 0] * y_attn + g[:, :, 1] * y_conv + g[:, :, 2] * y_pool
        x = x + merged @ w_o[l]

        h = rmsnorm(x, norm_ffn_g[l])
        gt, up = jnp.split(h @ w_ffn_in[l], 2, axis=-1)
        x = x + (jax.nn.silu(gt) * up) @ w_ffn_out[l]
    return x
```

```python
import functools
import math

import numpy as np
import jax
import jax.numpy as jnp
from jax import lax
from jax.experimental import pallas as pl
from jax.experimental.pallas import tpu as pltpu

D_MODEL = 1024
HEAD_DIM = 64
D_ATTN = D_MODEL // 2
N_HEADS = D_ATTN // HEAD_DIM
D_CONV = D_MODEL // 4
CONV_K = 3
D_POOL = D_MODEL // 4
POOL_WINDOWS = (2, 4, 8, 16)
POOL_GROUP_DIM = D_POOL // len(POOL_WINDOWS)
POOL_OUT_DIM = D_MODEL // len(POOL_WINDOWS)
D_FF = -(-8 * D_MODEL // (3 * 256)) * 256
EPS = 1e-6

LANES = 128
HALO = 16
N_PIECES = 3
LOG2E = 1.4426950408889634
NEG_BIG = -1e30

ROW_TILE = 512
Q_TILE = 512
K_TILE = 512
FF_CHUNKS = ((0, 1024), (1024, 2048), (2048, D_FF))
VMEM_LIMIT = 56 * 1024 * 1024

A_Q, A_K, A_V = 0, D_ATTN, 2 * D_ATTN
A_CX = 3 * D_ATTN
A_CB = A_CX + D_CONV
A_CC = A_CB + D_CONV
A_PX = A_CC + D_CONV
A_F = A_PX + D_POOL
A_COLS = A_F + LANES

X_ONES = N_HEADS * N_PIECES


def _bf16(x):
    return x.astype(jnp.bfloat16)


def _dot(a, b):
    return jnp.dot(a, b, preferred_element_type=jnp.float32)


def _split3(x):
    hi = _bf16(x)
    r1 = x - hi.astype(jnp.float32)
    mid = _bf16(r1)
    lo = _bf16(r1 - mid.astype(jnp.float32))
    return hi, mid, lo


def _rms_scale(x):
    return lax.rsqrt(jnp.mean(x * x, axis=-1, keepdims=True) + EPS)


def _proj_kernel(x_ref, g_ref, w_ref, fb_ref, gq_ref, gk_ref, eh_ref, tri_ref, pq_ref, pk_ref,
                 q_ref, k_ref, v_ref, qx_ref, kx_ref, cp_ref, carry_ref, *, tiles_per_seq):
    i = pl.program_id(0)

    @pl.when(i % tiles_per_seq == 0)
    def _():
        carry_ref[...] = jnp.zeros_like(carry_ref)

    x = x_ref[...]
    h = _bf16(x * _rms_scale(x) * g_ref[...])
    r = _dot(h, w_ref[...])

    q = r[:, A_Q:A_Q + D_ATTN]
    k = r[:, A_K:A_K + D_ATTN]
    ssq = _dot(_bf16(q * q), eh_ref[...])
    ssk = _dot(_bf16(k * k), eh_ref[...])
    q_ref[...] = _bf16(q * lax.rsqrt(ssq * (1.0 / HEAD_DIM) + EPS) * gq_ref[...])
    k_ref[...] = _bf16(k * lax.rsqrt(ssk * (1.0 / HEAD_DIM) + EPS) * gk_ref[...])
    v_ref[...] = _bf16(r[:, A_V:A_V + D_ATTN])

    cp_ref[:, 0:D_CONV] = _bf16(r[:, A_CC:A_CC + D_CONV] * r[:, A_CX:A_CX + D_CONV])
    cp_ref[:, D_CONV:2 * D_CONV] = _bf16(r[:, A_CB:A_CB + D_CONV])
    cp_ref[:, 2 * D_CONV:2 * D_CONV + D_POOL] = _bf16(r[:, A_PX:A_PX + D_POOL])

    f = r[:, A_F:A_F + LANES] + fb_ref[...]
    logf = jnp.minimum(f, 0.0) - jnp.log1p(jnp.exp(-jnp.abs(f)))
    lane = lax.broadcasted_iota(jnp.int32, logf.shape, 1)
    logf = jnp.where(lane < N_HEADS, logf, 0.0)
    pieces = jnp.concatenate(_split3(logf), axis=1)
    cum = _dot(tri_ref[...], pieces)
    c = cum[:, 0:LANES] + cum[:, LANES:2 * LANES] + cum[:, 2 * LANES:3 * LANES] + carry_ref[...]
    carry_ref[...] = c[c.shape[0] - 1:, :]

    ap = jnp.concatenate(_split3(jnp.where(lane == N_HEADS, 1.0, c * LOG2E)), axis=1)
    qx_ref[...] = _bf16(_dot(ap, pq_ref[...]))
    kx_ref[...] = _bf16(_dot(ap, pk_ref[...]))


def _const_spec(shape):
    return pl.BlockSpec(shape, lambda *_: (0,) * len(shape))


def _proj_call(x2, g, w_a, fb, gq, gk, eh, tri, pq, pk, *, seq):
    t = x2.shape[0]
    tm = ROW_TILE
    row = lambda i: (i, 0)
    outs = [
        jax.ShapeDtypeStruct((t, D_ATTN), jnp.bfloat16),
        jax.ShapeDtypeStruct((t, D_ATTN), jnp.bfloat16),
        jax.ShapeDtypeStruct((t, D_ATTN), jnp.bfloat16),
        jax.ShapeDtypeStruct((t, LANES), jnp.bfloat16),
        jax.ShapeDtypeStruct((t, LANES), jnp.bfloat16),
        jax.ShapeDtypeStruct((t, 2 * D_CONV + D_POOL), jnp.bfloat16),
    ]
    return pl.pallas_call(
        functools.partial(_proj_kernel, tiles_per_seq=seq // tm),
        grid=(t // tm,),
        in_specs=[
            pl.BlockSpec((tm, D_MODEL), row),
            _const_spec((1, D_MODEL)),
            _const_spec((D_MODEL, A_COLS)),
            _const_spec((1, LANES)),
            _const_spec((1, D_ATTN)),
            _const_spec((1, D_ATTN)),
            _const_spec((D_ATTN, D_ATTN)),
            _const_spec((tm, tm)),
            _const_spec((N_PIECES * LANES, LANES)),
            _const_spec((N_PIECES * LANES, LANES)),
        ],
        out_specs=[
            pl.BlockSpec((tm, D_ATTN), row),
            pl.BlockSpec((tm, D_ATTN), row),
            pl.BlockSpec((tm, D_ATTN), row),
            pl.BlockSpec((tm, LANES), row),
            pl.BlockSpec((tm, LANES), row),
            pl.BlockSpec((tm, 2 * D_CONV + D_POOL), row),
        ],
        out_shape=outs,
        scratch_shapes=[pltpu.VMEM((1, LANES), jnp.float32)],
        compiler_params=pltpu.CompilerParams(dimension_semantics=("arbitrary",), vmem_limit_bytes=VMEM_LIMIT),
        name="proj",
    )(x2, g, w_a, fb, gq, gk, eh, tri, pq, pk)


def _attn_kernel(q_ref, qx_ref, k_ref, kx_ref, v_ref, o_ref, m_sc, l_sc, acc_sc):
    pair = pl.program_id(1)
    i = pl.program_id(2)
    tq, tk = Q_TILE, K_TILE

    lane2 = lax.broadcasted_iota(jnp.int32, (1, 2 * LANES), 1)
    q2 = jnp.concatenate([q_ref[0], qx_ref[0]], axis=1)
    qh = []
    for hh in range(2):
        head = 2 * pair + hh
        lo = LANES + N_PIECES * head
        keep = ((lane2 >= hh * HEAD_DIM) & (lane2 < (hh + 1) * HEAD_DIM)) \
            | ((lane2 >= lo) & (lane2 < lo + N_PIECES)) \
            | ((lane2 >= lo + X_ONES) & (lane2 < lo + X_ONES + N_PIECES))
        qh.append(jnp.where(keep, q2, jnp.zeros_like(q2)))

    m_sc[...] = jnp.full_like(m_sc, NEG_BIG)
    l_sc[...] = jnp.zeros_like(l_sc)
    acc_sc[...] = jnp.zeros_like(acc_sc)

    def step(kt, masked):
        start = pl.multiple_of(kt * tk, tk)
        k2 = jnp.concatenate([k_ref[0, pl.ds(start, tk), :], kx_ref[0, pl.ds(start, tk), :]], axis=1)
        v = v_ref[0, pl.ds(start, tk), :]
        for hh in range(2):
            s = lax.dot_general(qh[hh], k2, (((1,), (1,)), ((), ())), preferred_element_type=jnp.float32)
            if masked:
                r_id = lax.broadcasted_iota(jnp.int32, s.shape, 0)
                c_id = lax.broadcasted_iota(jnp.int32, s.shape, 1)
                s = jnp.where(r_id >= c_id, s, NEG_BIG)
            m_prev = m_sc[hh]
            m_new = jnp.maximum(m_prev, jnp.max(s, axis=1, keepdims=True))
            alpha = jnp.exp2(m_prev - m_new)
            p = jnp.exp2(s - m_new)
            l_sc[hh] = alpha * l_sc[hh] + jnp.sum(p, axis=1, keepdims=True)
            acc_sc[hh] = alpha * acc_sc[hh] + _dot(_bf16(p), v)
            m_sc[hh] = m_new

    def body(kt, carry):
        step(kt, False)
        return carry

    lax.fori_loop(0, i, body, 0)
    step(i, True)

    lane = lax.broadcasted_iota(jnp.int32, (1, LANES), 1)
    o_a = acc_sc[0] / l_sc[0]
    o_b = acc_sc[1] / l_sc[1]
    o_ref[0] = _bf16(jnp.where(lane < HEAD_DIM, o_a, o_b))


def _attn_call(q, qx, k, kx, v):
    b, s, _ = q.shape
    tq = Q_TILE
    n_pairs = N_HEADS // 2
    return pl.pallas_call(
        _attn_kernel,
        grid=(b, n_pairs, s // tq),
        in_specs=[
            pl.BlockSpec((1, tq, LANES), lambda bi, p, i: (bi, i, p)),
            pl.BlockSpec((1, tq, LANES), lambda bi, p, i: (bi, i, 0)),
            pl.BlockSpec((1, s, LANES), lambda bi, p, i: (bi, 0, p)),
            pl.BlockSpec((1, s, LANES), lambda bi, p, i: (bi, 0, 0)),
            pl.BlockSpec((1, s, LANES), lambda bi, p, i: (bi, 0, p)),
        ],
        out_specs=pl.BlockSpec((1, tq, LANES), lambda bi, p, i: (bi, i, p)),
        out_shape=jax.ShapeDtypeStruct((b, s, D_ATTN), jnp.bfloat16),
        scratch_shapes=[
            pltpu.VMEM((2, tq, 1), jnp.float32),
            pltpu.VMEM((2, tq, 1), jnp.float32),
            pltpu.VMEM((2, tq, LANES), jnp.float32),
        ],
        compiler_params=pltpu.CompilerParams(
            dimension_semantics=("arbitrary", "arbitrary", "arbitrary"), vmem_limit_bytes=VMEM_LIMIT),
        name="attn",
    )(q, qx, k, kx, v)


def _shift_rows(x, k):
    return pltpu.roll(x, k, axis=0)


def _merge_kernel(x_ref, a_ref, cp_ref, halo_ref, g_ref, wg_ref, wao_ref, cw_ref, wco_ref, wpool_ref, ps_ref, wo_ref,
                  o_ref, *, tiles_per_seq):
    i = pl.program_id(0)
    tm = x_ref.shape[0]
    seq_tile = i % tiles_per_seq

    x = x_ref[...]
    h = _bf16(x * _rms_scale(x) * g_ref[...])

    halo = halo_ref[...].astype(jnp.float32) * jnp.where(seq_tile == 0, 0.0, 1.0)
    cp = cp_ref[...].astype(jnp.float32)
    z = jnp.concatenate([halo[:, 0:D_CONV], cp[:, 0:D_CONV]], axis=0)
    px = jnp.concatenate([halo[:, 2 * D_CONV:], cp[:, 2 * D_CONV:]], axis=0)
    cb = cp[:, D_CONV:2 * D_CONV]

    cw = cw_ref[...]
    conv = cw[2:3, :] * z + cw[1:2, :] * _shift_rows(z, 1) + cw[0:1, :] * _shift_rows(z, 2)
    u = _bf16(cb * conv[HALO:, :])
    y_conv = _dot(u, wco_ref[...])

    s2 = px + _shift_rows(px, 1)
    s4 = s2 + _shift_rows(s2, 2)
    s8 = s4 + _shift_rows(s4, 4)
    s16 = s8 + _shift_rows(s8, 8)
    grp = lax.broadcasted_iota(jnp.int32, (tm, D_POOL), 1) // POOL_GROUP_DIM
    wsum = jnp.where(grp == 0, s2[HALO:], jnp.where(grp == 1, s4[HALO:], jnp.where(grp == 2, s8[HALO:], s16[HALO:])))
    pos = seq_tile * tm + lax.broadcasted_iota(jnp.int32, (tm, D_POOL), 0)
    win = jnp.left_shift(2, grp)
    counts = jnp.minimum(pos + 1, win).astype(jnp.float32)
    d = _bf16(wsum / counts - px[HALO:])
    y_pool = _dot(d, wpool_ref[...]) * ps_ref[...]

    y_attn = _dot(a_ref[...], wao_ref[...])

    merged = jax.nn.sigmoid(_dot(h, wg_ref[:, 0:D_MODEL])) * y_attn
    merged += jax.nn.sigmoid(_dot(h, wg_ref[:, D_MODEL:2 * D_MODEL])) * y_conv
    merged += jax.nn.sigmoid(_dot(h, wg_ref[:, 2 * D_MODEL:3 * D_MODEL])) * y_pool
    o_ref[...] = x + _dot(_bf16(merged), wo_ref[...])


def _merge_call(x2, a2, cp, g, wg, wao, cw, wco, wpool, ps, wo, *, seq):
    t = x2.shape[0]
    tm = ROW_TILE
    row = lambda i: (i, 0)
    cpw = 2 * D_CONV + D_POOL
    halo_blocks = tm // HALO
    return pl.pallas_call(
        functools.partial(_merge_kernel, tiles_per_seq=seq // tm),
        grid=(t // tm,),
        in_specs=[
            pl.BlockSpec((tm, D_MODEL), row),
            pl.BlockSpec((tm, D_ATTN), row),
            pl.BlockSpec((tm, cpw), row),
            pl.BlockSpec((HALO, cpw), lambda i: (jnp.maximum(i * halo_blocks - 1, 0), 0)),
            _const_spec((1, D_MODEL)),
            _const_spec((D_MODEL, 3 * D_MODEL)),
            _const_spec((D_ATTN, D_MODEL)),
            _const_spec((8, D_CONV)),
            _const_spec((D_CONV, D_MODEL)),
            _const_spec((D_POOL, D_MODEL)),
            _const_spec((1, D_MODEL)),
            _const_spec((D_MODEL, D_MODEL)),
        ],
        out_specs=pl.BlockSpec((tm, D_MODEL), row),
        out_shape=jax.ShapeDtypeStruct((t, D_MODEL), jnp.float32),
        compiler_params=pltpu.CompilerParams(dimension_semantics=("arbitrary",), vmem_limit_bytes=VMEM_LIMIT),
        name="merge",
    )(x2, a2, cp, cp, g, wg, wao, cw, wco, wpool, ps, wo)


def _ffn_kernel(x_ref, g_ref, wi_ref, wo_ref, o_ref):
    x = x_ref[...]
    h = _bf16(x * _rms_scale(x) * g_ref[...])
    acc = x
    for c0, c1 in FF_CHUNKS:
        gt = _dot(h, wi_ref[:, c0:c1])
        up = _dot(h, wi_ref[:, D_FF + c0:D_FF + c1])
        act = _bf16(gt * jax.nn.sigmoid(gt) * up)
        acc = acc + _dot(act, wo_ref[c0:c1, :])
    o_ref[...] = acc


def _ffn_call(x2, g, wi, wo):
    t = x2.shape[0]
    tm = ROW_TILE
    row = lambda i: (i, 0)
    return pl.pallas_call(
        _ffn_kernel,
        grid=(t // tm,),
        in_specs=[
            pl.BlockSpec((tm, D_MODEL), row),
            _const_spec((1, D_MODEL)),
            _const_spec((D_MODEL, 2 * D_FF)),
            _const_spec((D_FF, D_MODEL)),
        ],
        out_specs=pl.BlockSpec((tm, D_MODEL), row),
        out_shape=jax.ShapeDtypeStruct((t, D_MODEL), jnp.float32),
        compiler_params=pltpu.CompilerParams(dimension_semantics=("arbitrary",), vmem_limit_bytes=VMEM_LIMIT),
        name="ffn",
    )(x2, g, wi, wo)


def _placement_matrices():
    pq = np.zeros((N_PIECES * LANES, LANES), np.float32)
    pk = np.zeros((N_PIECES * LANES, LANES), np.float32)
    ones_row = N_HEADS
    for h in range(N_HEADS):
        for j in range(N_PIECES):
            pq[j * LANES + h, N_PIECES * h + j] = 1.0
            pk[ones_row, N_PIECES * h + j] = 1.0
            pq[ones_row, X_ONES + N_PIECES * h + j] = 1.0
            pk[j * LANES + h, X_ONES + N_PIECES * h + j] = -1.0
    return jnp.asarray(pq, jnp.bfloat16), jnp.asarray(pk, jnp.bfloat16)


def kernel(x, norm_mix_g, w_in, forget_b, q_norm_g, k_norm_g, w_attn_out, conv_w, w_conv_out, pool_w, pool_scale,
           w_o, norm_ffn_g, w_ffn_in, w_ffn_out):
    bsz, seq, _ = x.shape
    depth = w_in.shape[0]
    assert seq % ROW_TILE == 0 and seq % Q_TILE == 0 and Q_TILE == K_TILE

    bf = jnp.bfloat16
    eh = jnp.asarray(np.kron(np.eye(N_HEADS), np.ones((HEAD_DIM, HEAD_DIM))), bf)
    tri = jnp.asarray(np.tril(np.ones((ROW_TILE, ROW_TILE))), bf)
    pq, pk = _placement_matrices()

    o_f = 3 * D_ATTN
    o_cx = o_f + N_HEADS
    o_g = o_cx + 3 * D_CONV + D_POOL
    w_qkv = w_in[:, :, :o_f]
    w_f = jnp.pad(w_in[:, :, o_f:o_cx], ((0, 0), (0, 0), (0, LANES - N_HEADS)))
    w_a = jnp.concatenate([w_qkv, w_in[:, :, o_cx:o_g], w_f], axis=2).astype(bf)
    w_g = w_in[:, :, o_g:].astype(bf)
    fb = jnp.pad(forget_b, ((0, 0), (0, LANES - N_HEADS)))[:, None, :]
    q_scale = (HEAD_DIM ** -0.5) * LOG2E
    gq = jnp.tile(q_norm_g, (1, N_HEADS))[:, None, :] * q_scale
    gk = jnp.tile(k_norm_g, (1, N_HEADS))[:, None, :]
    wao = w_attn_out.astype(bf)
    cw = jnp.pad(conv_w, ((0, 0), (0, 8 - CONV_K), (0, 0)))
    wco = w_conv_out.astype(bf)
    grp_mask = jnp.asarray(np.kron(np.eye(len(POOL_WINDOWS)), np.ones((POOL_GROUP_DIM, POOL_OUT_DIM))), jnp.float32)
    wpool = (jnp.tile(pool_w.reshape(depth, D_POOL, POOL_OUT_DIM), (1, 1, len(POOL_WINDOWS))) * grp_mask).astype(bf)
    wo = w_o.astype(bf)
    wfi = w_ffn_in.astype(bf)
    wfo = w_ffn_out.astype(bf)

    x2 = x.reshape(bsz * seq, D_MODEL)
    for l in range(depth):
        q, k, v, qx, kx, cp = _proj_call(x2, norm_mix_g[l][None], w_a[l], fb[l], gq[l], gk[l], eh, tri, pq, pk, seq=seq)
        shp = lambda t: t.reshape(bsz, seq, t.shape[-1])
        a = _attn_call(shp(q), shp(qx), shp(k), shp(kx), shp(v))
        x2 = _merge_call(x2, a.reshape(bsz * seq, D_ATTN), cp, norm_mix_g[l][None], w_g[l], wao[l], cw[l], wco[l],
                         wpool[l], pool_scale[l][None], wo[l], seq=seq)
        x2 = _ffn_call(x2, norm_ffn_g[l][None], wfi[l], wfo[l])
    return x2.reshape(bsz, seq, D_MODEL)
```

```python
import functools
import math

import numpy as np
import jax
import jax.numpy as jnp
from jax import lax
from jax.experimental import pallas as pl
from jax.experimental.pallas import tpu as pltpu

D_MODEL = 1024
HEAD_DIM = 64
D_ATTN = D_MODEL // 2
N_HEADS = D_ATTN // HEAD_DIM
D_CONV = D_MODEL // 4
CONV_K = 3
D_POOL = D_MODEL // 4
POOL_WINDOWS = (2, 4, 8, 16)
POOL_GROUP_DIM = D_POOL // len(POOL_WINDOWS)
POOL_OUT_DIM = D_MODEL // len(POOL_WINDOWS)
D_FF = -(-8 * D_MODEL // (3 * 256)) * 256
EPS = 1e-6

LANES = 128
HALO = 16
N_PIECES = 3
LOG2E = 1.4426950408889634
NEG_BIG = -1e30
UNSHIFTED_LOGIT_LIMIT = 100.0

ROW_TILE = 512
Q_TILE = 512
K_TILE = 512
FF_CHUNKS = ((0, 1024), (1024, 2048), (2048, D_FF))
VMEM_LIMIT = 56 * 1024 * 1024

A_Q, A_K, A_V = 0, D_ATTN, 2 * D_ATTN
A_CX = 3 * D_ATTN
A_CB = A_CX + D_CONV
A_CC = A_CB + D_CONV
A_PX = A_CC + D_CONV
A_F = A_PX + D_POOL
A_COLS = A_F + LANES

X_ONES = N_HEADS * N_PIECES


def _bf16(x):
    return x.astype(jnp.bfloat16)


def _dot(a, b):
    return jnp.dot(a, b, preferred_element_type=jnp.float32)


def _split3(x):
    hi = _bf16(x)
    r1 = x - hi.astype(jnp.float32)
    mid = _bf16(r1)
    lo = _bf16(r1 - mid.astype(jnp.float32))
    return hi, mid, lo


def _rms_scale(x):
    return lax.rsqrt(jnp.mean(x * x, axis=-1, keepdims=True) + EPS)


def _proj_kernel(x_ref, g_ref, w_ref, fb_ref, gq_ref, gk_ref, eh_ref, tri_ref, pq_ref, pk_ref,
                 q_ref, k_ref, v_ref, qx_ref, kx_ref, cp_ref, carry_ref, *, tiles_per_seq):
    i = pl.program_id(0)

    @pl.when(i % tiles_per_seq == 0)
    def _():
        carry_ref[...] = jnp.zeros_like(carry_ref)

    x = x_ref[...]
    h = _bf16(x * _rms_scale(x) * g_ref[...])
    r = _dot(h, w_ref[...])

    q = r[:, A_Q:A_Q + D_ATTN]
    k = r[:, A_K:A_K + D_ATTN]
    ssq = _dot(_bf16(q * q), eh_ref[...])
    ssk = _dot(_bf16(k * k), eh_ref[...])
    q_ref[...] = _bf16(q * lax.rsqrt(ssq * (1.0 / HEAD_DIM) + EPS) * gq_ref[...])
    k_ref[...] = _bf16(k * lax.rsqrt(ssk * (1.0 / HEAD_DIM) + EPS) * gk_ref[...])
    ones_col = _bf16(jnp.where(lax.broadcasted_iota(jnp.int32, (q.shape[0], LANES - HEAD_DIM), 1) == 0, 1.0, 0.0))
    for hd in range(N_HEADS):
        v_ref[:, hd * LANES:hd * LANES + HEAD_DIM] = _bf16(r[:, A_V + hd * HEAD_DIM:A_V + (hd + 1) * HEAD_DIM])
        v_ref[:, hd * LANES + HEAD_DIM:(hd + 1) * LANES] = ones_col

    cp_ref[:, 0:D_CONV] = _bf16(r[:, A_CC:A_CC + D_CONV] * r[:, A_CX:A_CX + D_CONV])
    cp_ref[:, D_CONV:2 * D_CONV] = _bf16(r[:, A_CB:A_CB + D_CONV])
    cp_ref[:, 2 * D_CONV:2 * D_CONV + D_POOL] = _bf16(r[:, A_PX:A_PX + D_POOL])

    f = r[:, A_F:A_F + LANES] + fb_ref[...]
    logf = jnp.minimum(f, 0.0) - jnp.log1p(jnp.exp(-jnp.abs(f)))
    lane = lax.broadcasted_iota(jnp.int32, logf.shape, 1)
    logf = jnp.where(lane < N_HEADS, logf, 0.0)
    pieces = jnp.concatenate(_split3(logf), axis=1)
    cum = _dot(tri_ref[...], pieces)
    c = cum[:, 0:LANES] + cum[:, LANES:2 * LANES] + cum[:, 2 * LANES:3 * LANES] + carry_ref[...]
    carry_ref[...] = c[c.shape[0] - 1:, :]

    ap = jnp.concatenate(_split3(jnp.where(lane == N_HEADS, 1.0, c * LOG2E)), axis=1)
    qx_ref[...] = _bf16(_dot(ap, pq_ref[...]))
    kx_ref[...] = _bf16(_dot(ap, pk_ref[...]))


def _const_spec(shape):
    return pl.BlockSpec(shape, lambda *_: (0,) * len(shape))


def _proj_call(x2, g, w_a, fb, gq, gk, eh, tri, pq, pk, *, seq):
    t = x2.shape[0]
    tm = ROW_TILE
    row = lambda i: (i, 0)
    outs = [
        jax.ShapeDtypeStruct((t, D_ATTN), jnp.bfloat16),
        jax.ShapeDtypeStruct((t, D_ATTN), jnp.bfloat16),
        jax.ShapeDtypeStruct((t, N_HEADS * LANES), jnp.bfloat16),
        jax.ShapeDtypeStruct((t, LANES), jnp.bfloat16),
        jax.ShapeDtypeStruct((t, LANES), jnp.bfloat16),
        jax.ShapeDtypeStruct((t, 2 * D_CONV + D_POOL), jnp.bfloat16),
    ]
    return pl.pallas_call(
        functools.partial(_proj_kernel, tiles_per_seq=seq // tm),
        grid=(t // tm,),
        in_specs=[
            pl.BlockSpec((tm, D_MODEL), row),
            _const_spec((1, D_MODEL)),
            _const_spec((D_MODEL, A_COLS)),
            _const_spec((1, LANES)),
            _const_spec((1, D_ATTN)),
            _const_spec((1, D_ATTN)),
            _const_spec((D_ATTN, D_ATTN)),
            _const_spec((tm, tm)),
            _const_spec((N_PIECES * LANES, LANES)),
            _const_spec((N_PIECES * LANES, LANES)),
        ],
        out_specs=[
            pl.BlockSpec((tm, D_ATTN), row),
            pl.BlockSpec((tm, D_ATTN), row),
            pl.BlockSpec((tm, N_HEADS * LANES), row),
            pl.BlockSpec((tm, LANES), row),
            pl.BlockSpec((tm, LANES), row),
            pl.BlockSpec((tm, 2 * D_CONV + D_POOL), row),
        ],
        out_shape=outs,
        scratch_shapes=[pltpu.VMEM((1, LANES), jnp.float32)],
        compiler_params=pltpu.CompilerParams(dimension_semantics=("arbitrary",), vmem_limit_bytes=VMEM_LIMIT),
        name="proj",
    )(x2, g, w_a, fb, gq, gk, eh, tri, pq, pk)


def _attn_kernel(q_ref, qx_ref, k_ref, kx_ref, v_ref, o_ref, acc_sc, *m_sc, online):
    pair = pl.program_id(1)
    i = pl.program_id(2)
    tq, tk = Q_TILE, K_TILE

    lane2 = lax.broadcasted_iota(jnp.int32, (1, 2 * LANES), 1)
    q2 = jnp.concatenate([q_ref[0], qx_ref[0]], axis=1)
    qh = []
    for hh in range(2):
        lo = LANES + N_PIECES * (2 * pair + hh)
        keep = ((lane2 >= hh * HEAD_DIM) & (lane2 < (hh + 1) * HEAD_DIM)) \
            | ((lane2 >= lo) & (lane2 < lo + N_PIECES)) \
            | ((lane2 >= lo + X_ONES) & (lane2 < lo + X_ONES + N_PIECES))
        qh.append(jnp.where(keep, q2, jnp.zeros_like(q2)))

    acc_sc[...] = jnp.zeros_like(acc_sc)
    if online:
        m_sc[0][...] = jnp.full_like(m_sc[0], NEG_BIG)

    def step(kt, masked):
        start = pl.multiple_of(kt * tk, tk)
        k2 = jnp.concatenate([k_ref[0, pl.ds(start, tk), :], kx_ref[0, pl.ds(start, tk), :]], axis=1)
        for hh in range(2):
            v = v_ref[0, pl.ds(start, tk), hh * LANES:(hh + 1) * LANES]
            s = lax.dot_general(qh[hh], k2, (((1,), (1,)), ((), ())), preferred_element_type=jnp.float32)
            if masked:
                r_id = lax.broadcasted_iota(jnp.int32, s.shape, 0)
                c_id = lax.broadcasted_iota(jnp.int32, s.shape, 1)
                s = jnp.where(r_id >= c_id, s, NEG_BIG)
            if online:
                m_prev = m_sc[0][hh]
                m_new = jnp.maximum(m_prev, jnp.max(s, axis=1, keepdims=True))
                m_sc[0][hh] = m_new
                acc_sc[hh] = jnp.exp2(m_prev - m_new) * acc_sc[hh] + _dot(_bf16(jnp.exp2(s - m_new)), v)
            else:
                acc_sc[hh] += _dot(_bf16(jnp.exp2(s)), v)

    def body(kt, carry):
        step(kt, False)
        return carry

    lax.fori_loop(0, i, body, 0)
    step(i, True)

    lane = lax.broadcasted_iota(jnp.int32, (1, LANES), 1)
    acc_a, acc_b = acc_sc[0], acc_sc[1]
    o_a = acc_a / acc_a[:, HEAD_DIM:HEAD_DIM + 1]
    o_b = acc_b / acc_b[:, HEAD_DIM:HEAD_DIM + 1]
    o_ref[0] = _bf16(jnp.where(lane < HEAD_DIM, o_a, pltpu.roll(o_b, HEAD_DIM, axis=1)))


def _attn_call(q, qx, k, kx, v, *, online):
    b, s, _ = q.shape
    tq = Q_TILE
    n_pairs = N_HEADS // 2
    scratch = [pltpu.VMEM((2, tq, LANES), jnp.float32)]
    if online:
        scratch.append(pltpu.VMEM((2, tq, 1), jnp.float32))
    return pl.pallas_call(
        functools.partial(_attn_kernel, online=online),
        grid=(b, n_pairs, s // tq),
        in_specs=[
            pl.BlockSpec((1, tq, LANES), lambda bi, p, i: (bi, i, p)),
            pl.BlockSpec((1, tq, LANES), lambda bi, p, i: (bi, i, 0)),
            pl.BlockSpec((1, s, LANES), lambda bi, p, i: (bi, 0, p)),
            pl.BlockSpec((1, s, LANES), lambda bi, p, i: (bi, 0, 0)),
            pl.BlockSpec((1, s, 2 * LANES), lambda bi, p, i: (bi, 0, p)),
        ],
        out_specs=pl.BlockSpec((1, tq, LANES), lambda bi, p, i: (bi, i, p)),
        out_shape=jax.ShapeDtypeStruct((b, s, D_ATTN), jnp.bfloat16),
        scratch_shapes=scratch,
        compiler_params=pltpu.CompilerParams(
            dimension_semantics=("arbitrary", "arbitrary", "arbitrary"), vmem_limit_bytes=VMEM_LIMIT),
        name="attn_online" if online else "attn",
    )(q, qx, k, kx, v)


def _shift_rows(x, k):
    return pltpu.roll(x, k, axis=0)


def _merge_kernel(x_ref, a_ref, cp_ref, halo_ref, g_ref, wg_ref, wao_ref, cw_ref, wco_ref, wpool_ref, ps_ref, wo_ref,
                  o_ref, *, tiles_per_seq):
    i = pl.program_id(0)
    tm = x_ref.shape[0]
    seq_tile = i % tiles_per_seq

    x = x_ref[...]
    h = _bf16(x * _rms_scale(x) * g_ref[...])

    halo = halo_ref[...].astype(jnp.float32) * jnp.where(seq_tile == 0, 0.0, 1.0)
    cp = cp_ref[...].astype(jnp.float32)
    z = jnp.concatenate([halo[:, 0:D_CONV], cp[:, 0:D_CONV]], axis=0)
    px = jnp.concatenate([halo[:, 2 * D_CONV:], cp[:, 2 * D_CONV:]], axis=0)
    cb = cp[:, D_CONV:2 * D_CONV]

    cw = cw_ref[...]
    conv = cw[2:3, :] * z + cw[1:2, :] * _shift_rows(z, 1) + cw[0:1, :] * _shift_rows(z, 2)
    u = _bf16(cb * conv[HALO:, :])
    y_conv = _dot(u, wco_ref[...])

    s2 = px + _shift_rows(px, 1)
    s4 = s2 + _shift_rows(s2, 2)
    s8 = s4 + _shift_rows(s4, 4)
    s16 = s8 + _shift_rows(s8, 8)
    grp = lax.broadcasted_iota(jnp.int32, (tm, D_POOL), 1) // POOL_GROUP_DIM
    wsum = jnp.where(grp == 0, s2[HALO:], jnp.where(grp == 1, s4[HALO:], jnp.where(grp == 2, s8[HALO:], s16[HALO:])))
    pos = seq_tile * tm + lax.broadcasted_iota(jnp.int32, (tm, D_POOL), 0)
    win = jnp.left_shift(2, grp)
    counts = jnp.minimum(pos + 1, win).astype(jnp.float32)
    d = _bf16(wsum / counts - px[HALO:])
    y_pool = _dot(d, wpool_ref[...]) * ps_ref[...]

    y_attn = _dot(a_ref[...], wao_ref[...])

    merged = jax.nn.sigmoid(_dot(h, wg_ref[:, 0:D_MODEL])) * y_attn
    merged += jax.nn.sigmoid(_dot(h, wg_ref[:, D_MODEL:2 * D_MODEL])) * y_conv
    merged += jax.nn.sigmoid(_dot(h, wg_ref[:, 2 * D_MODEL:3 * D_MODEL])) * y_pool
    o_ref[...] = x + _dot(_bf16(merged), wo_ref[...])


def _merge_call(x2, a2, cp, g, wg, wao, cw, wco, wpool, ps, wo, *, seq):
    t = x2.shape[0]
    tm = ROW_TILE
    row = lambda i: (i, 0)
    cpw = 2 * D_CONV + D_POOL
    halo_blocks = tm // HALO
    return pl.pallas_call(
        functools.partial(_merge_kernel, tiles_per_seq=seq // tm),
        grid=(t // tm,),
        in_specs=[
            pl.BlockSpec((tm, D_MODEL), row),
            pl.BlockSpec((tm, D_ATTN), row),
            pl.BlockSpec((tm, cpw), row),
            pl.BlockSpec((HALO, cpw), lambda i: (jnp.maximum(i * halo_blocks - 1, 0), 0)),
            _const_spec((1, D_MODEL)),
            _const_spec((D_MODEL, 3 * D_MODEL)),
            _const_spec((D_ATTN, D_MODEL)),
            _const_spec((8, D_CONV)),
            _const_spec((D_CONV, D_MODEL)),
            _const_spec((D_POOL, D_MODEL)),
            _const_spec((1, D_MODEL)),
            _const_spec((D_MODEL, D_MODEL)),
        ],
        out_specs=pl.BlockSpec((tm, D_MODEL), row),
        out_shape=jax.ShapeDtypeStruct((t, D_MODEL), jnp.float32),
        compiler_params=pltpu.CompilerParams(dimension_semantics=("arbitrary",), vmem_limit_bytes=VMEM_LIMIT),
        name="merge",
    )(x2, a2, cp, cp, g, wg, wao, cw, wco, wpool, ps, wo)


def _ffn_kernel(x_ref, g_ref, wi_ref, wo_ref, o_ref):
    x = x_ref[...]
    h = _bf16(x * _rms_scale(x) * g_ref[...])
    acc = x
    for c0, c1 in FF_CHUNKS:
        gt = _dot(h, wi_ref[:, c0:c1])
        up = _dot(h, wi_ref[:, D_FF + c0:D_FF + c1])
        act = _bf16(gt * jax.nn.sigmoid(gt) * up)
        acc = acc + _dot(act, wo_ref[c0:c1, :])
    o_ref[...] = acc


def _ffn_call(x2, g, wi, wo):
    t = x2.shape[0]
    tm = ROW_TILE
    row = lambda i: (i, 0)
    return pl.pallas_call(
        _ffn_kernel,
        grid=(t // tm,),
        in_specs=[
            pl.BlockSpec((tm, D_MODEL), row),
            _const_spec((1, D_MODEL)),
            _const_spec((D_MODEL, 2 * D_FF)),
            _const_spec((D_FF, D_MODEL)),
        ],
        out_specs=pl.BlockSpec((tm, D_MODEL), row),
        out_shape=jax.ShapeDtypeStruct((t, D_MODEL), jnp.float32),
        compiler_params=pltpu.CompilerParams(dimension_semantics=("arbitrary",), vmem_limit_bytes=VMEM_LIMIT),
        name="ffn",
    )(x2, g, wi, wo)


def _placement_matrices():
    pq = np.zeros((N_PIECES * LANES, LANES), np.float32)
    pk = np.zeros((N_PIECES * LANES, LANES), np.float32)
    ones_row = N_HEADS
    for h in range(N_HEADS):
        for j in range(N_PIECES):
            pq[j * LANES + h, N_PIECES * h + j] = 1.0
            pk[ones_row, N_PIECES * h + j] = 1.0
            pq[ones_row, X_ONES + N_PIECES * h + j] = 1.0
            pk[j * LANES + h, X_ONES + N_PIECES * h + j] = -1.0
    return jnp.asarray(pq, jnp.bfloat16), jnp.asarray(pk, jnp.bfloat16)


def kernel(x, norm_mix_g, w_in, forget_b, q_norm_g, k_norm_g, w_attn_out, conv_w, w_conv_out, pool_w, pool_scale,
           w_o, norm_ffn_g, w_ffn_in, w_ffn_out):
    bsz, seq, _ = x.shape
    depth = w_in.shape[0]
    assert seq % ROW_TILE == 0 and seq % Q_TILE == 0 and Q_TILE == K_TILE

    bf = jnp.bfloat16
    eh = jnp.asarray(np.kron(np.eye(N_HEADS), np.ones((HEAD_DIM, HEAD_DIM))), bf)
    tri = jnp.asarray(np.tril(np.ones((ROW_TILE, ROW_TILE))), bf)
    pq, pk = _placement_matrices()

    o_f = 3 * D_ATTN
    o_cx = o_f + N_HEADS
    o_g = o_cx + 3 * D_CONV + D_POOL
    w_qkv = w_in[:, :, :o_f]
    w_f = jnp.pad(w_in[:, :, o_f:o_cx], ((0, 0), (0, 0), (0, LANES - N_HEADS)))
    w_a = jnp.concatenate([w_qkv, w_in[:, :, o_cx:o_g], w_f], axis=2).astype(bf)
    w_g = w_in[:, :, o_g:].astype(bf)
    fb = jnp.pad(forget_b, ((0, 0), (0, LANES - N_HEADS)))[:, None, :]
    q_scale = (HEAD_DIM ** -0.5) * LOG2E
    gq = jnp.tile(q_norm_g, (1, N_HEADS))[:, None, :] * q_scale
    gk = jnp.tile(k_norm_g, (1, N_HEADS))[:, None, :]
    wao = w_attn_out.astype(bf)
    cw = jnp.pad(conv_w, ((0, 0), (0, 8 - CONV_K), (0, 0)))
    wco = w_conv_out.astype(bf)
    grp_mask = jnp.asarray(np.kron(np.eye(len(POOL_WINDOWS)), np.ones((POOL_GROUP_DIM, POOL_OUT_DIM))), jnp.float32)
    wpool = (jnp.tile(pool_w.reshape(depth, D_POOL, POOL_OUT_DIM), (1, 1, len(POOL_WINDOWS))) * grp_mask).astype(bf)
    wo = w_o.astype(bf)
    wfi = w_ffn_in.astype(bf)
    wfo = w_ffn_out.astype(bf)

    x2 = x.reshape(bsz * seq, D_MODEL)
    for l in range(depth):
        q, k, v, qx, kx, cp = _proj_call(x2, norm_mix_g[l][None], w_a[l], fb[l], gq[l], gk[l], eh, tri, pq, pk, seq=seq)
        shp = lambda t: t.reshape(bsz, seq, t.shape[-1])
        logit_bound = HEAD_DIM * jnp.max(jnp.abs(gq[l])) * jnp.max(jnp.abs(gk[l]))
        a = lax.cond(logit_bound < UNSHIFTED_LOGIT_LIMIT,
                     functools.partial(_attn_call, online=False), functools.partial(_attn_call, online=True),
                     shp(q), shp(qx), shp(k), shp(kx), shp(v))
        x2 = _merge_call(x2, a.reshape(bsz * seq, D_ATTN), cp, norm_mix_g[l][None], w_g[l], wao[l], cw[l], wco[l],
                         wpool[l], pool_scale[l][None], wo[l], seq=seq)
        x2 = _ffn_call(x2, norm_ffn_g[l][None], wfi[l], wfo[l])
    return x2.reshape(bsz, seq, D_MODEL)
```

```python
import functools
import math

import numpy as np
import jax
import jax.numpy as jnp
from jax import lax
from jax.experimental import pallas as pl
from jax.experimental.pallas import tpu as pltpu

D_MODEL = 1024
HEAD_DIM = 64
D_ATTN = D_MODEL // 2
N_HEADS = D_ATTN // HEAD_DIM
D_CONV = D_MODEL // 4
CONV_K = 3
D_POOL = D_MODEL // 4
POOL_WINDOWS = (2, 4, 8, 16)
POOL_GROUP_DIM = D_POOL // len(POOL_WINDOWS)
POOL_OUT_DIM = D_MODEL // len(POOL_WINDOWS)
D_FF = -(-8 * D_MODEL // (3 * 256)) * 256
EPS = 1e-6

LANES = 128
VT_ROWS = HEAD_DIM + 16
HALO = 16
N_PIECES = 3
LOG2E = 1.4426950408889634
NEG_BIG = -1e30
UNSHIFTED_LOGIT_LIMIT = 100.0

ROW_TILE = 512
Q_TILE = 512
K_TILE = 512
FF_CHUNKS = ((0, 1024), (1024, 2048), (2048, D_FF))
VMEM_LIMIT = 56 * 1024 * 1024

A_Q, A_K, A_V = 0, D_ATTN, 2 * D_ATTN
A_CX = 3 * D_ATTN
A_CB = A_CX + D_CONV
A_CC = A_CB + D_CONV
A_PX = A_CC + D_CONV
A_F = A_PX + D_POOL
A_COLS = A_F + LANES

X_ONES = N_HEADS * N_PIECES


def _bf16(x):
    return x.astype(jnp.bfloat16)


def _dot(a, b):
    return jnp.dot(a, b, preferred_element_type=jnp.float32)


def _split3(x):
    hi = _bf16(x)
    r1 = x - hi.astype(jnp.float32)
    mid = _bf16(r1)
    lo = _bf16(r1 - mid.astype(jnp.float32))
    return hi, mid, lo


def _rms_scale(x):
    return lax.rsqrt(jnp.mean(x * x, axis=-1, keepdims=True) + EPS)


def _proj_kernel(x_ref, g_ref, w_ref, fb_ref, gq_ref, gk_ref, eh_ref, tri_ref, pq_ref, pk_ref,
                 qt_ref, kh_ref, vt_ref, cp_ref, carry_ref, *, tiles_per_seq):
    i = pl.program_id(0)

    @pl.when(i % tiles_per_seq == 0)
    def _():
        carry_ref[...] = jnp.zeros_like(carry_ref)

    x = x_ref[...]
    h = _bf16(x * _rms_scale(x) * g_ref[...])
    r = _dot(h, w_ref[...])
    tm = r.shape[0]
    lane = lax.broadcasted_iota(jnp.int32, (tm, LANES), 1)

    f = r[:, A_F:A_F + LANES] + fb_ref[...]
    logf = jnp.minimum(f, 0.0) - jnp.log1p(jnp.exp(-jnp.abs(f)))
    logf = jnp.where(lane < N_HEADS, logf, 0.0)
    pieces = jnp.concatenate(_split3(logf), axis=1)
    cum = _dot(tri_ref[...], pieces)
    c = cum[:, 0:LANES] + cum[:, LANES:2 * LANES] + cum[:, 2 * LANES:3 * LANES] + carry_ref[...]
    carry_ref[...] = c[tm - 1:, :]

    ap = jnp.concatenate(_split3(jnp.where(lane == N_HEADS, 1.0, c * LOG2E)), axis=1)
    exq = _dot(ap, pq_ref[...])
    exk = _dot(ap, pk_ref[...])

    q = r[:, A_Q:A_Q + D_ATTN]
    k = r[:, A_K:A_K + D_ATTN]
    ssq = _dot(_bf16(q * q), eh_ref[...])
    ssk = _dot(_bf16(k * k), eh_ref[...])
    qn = q * lax.rsqrt(ssq * (1.0 / HEAD_DIM) + EPS) * gq_ref[...]
    kn = k * lax.rsqrt(ssk * (1.0 / HEAD_DIM) + EPS) * gk_ref[...]

    ex_keep = ((lane >= HEAD_DIM) & (lane < HEAD_DIM + N_PIECES)) \
        | ((lane >= HEAD_DIM + X_ONES) & (lane < HEAD_DIM + X_ONES + N_PIECES))
    for hd in range(N_HEADS):
        sl = slice((hd // 2) * LANES, (hd // 2 + 1) * LANES)
        blk_q, blk_k = qn[:, sl], kn[:, sl]
        if hd % 2:
            blk_q = pltpu.roll(blk_q, HEAD_DIM, axis=1)
            blk_k = pltpu.roll(blk_k, HEAD_DIM, axis=1)
        shift = HEAD_DIM - N_PIECES * hd
        qh = jnp.where(lane < HEAD_DIM, blk_q, jnp.where(ex_keep, pltpu.roll(exq, shift, axis=1), 0.0))
        kh = jnp.where(lane < HEAD_DIM, blk_k, jnp.where(ex_keep, pltpu.roll(exk, shift, axis=1), 0.0))
        qt_ref[0, 0, hd * LANES:(hd + 1) * LANES, :] = _bf16(qh.T)
        kh_ref[:, hd * LANES:(hd + 1) * LANES] = _bf16(kh)

    ones_rows = _bf16(jnp.where(lax.broadcasted_iota(jnp.int32, (VT_ROWS - HEAD_DIM, tm), 0) == 0, 1.0, 0.0))
    for pr in range(N_HEADS // 2):
        vt = _bf16(r[:, A_V + pr * LANES:A_V + (pr + 1) * LANES].T)
        for hh in range(2):
            base = (2 * pr + hh) * VT_ROWS
            vt_ref[0, 0, base:base + HEAD_DIM, :] = vt[hh * HEAD_DIM:(hh + 1) * HEAD_DIM, :]
            vt_ref[0, 0, base + HEAD_DIM:base + VT_ROWS, :] = ones_rows

    cp_ref[:, 0:D_CONV] = _bf16(r[:, A_CC:A_CC + D_CONV] * r[:, A_CX:A_CX + D_CONV])
    cp_ref[:, D_CONV:2 * D_CONV] = _bf16(r[:, A_CB:A_CB + D_CONV])
    cp_ref[:, 2 * D_CONV:2 * D_CONV + D_POOL] = _bf16(r[:, A_PX:A_PX + D_POOL])


def _const_spec(shape):
    return pl.BlockSpec(shape, lambda *_: (0,) * len(shape))


def _proj_call(x2, g, w_a, fb, gq, gk, eh, tri, pq, pk, *, seq):
    t = x2.shape[0]
    tm = ROW_TILE
    row = lambda i: (i, 0)
    outs = [
        jax.ShapeDtypeStruct((t // seq, seq // tm, N_HEADS * LANES, tm), jnp.bfloat16),
        jax.ShapeDtypeStruct((t, N_HEADS * LANES), jnp.bfloat16),
        jax.ShapeDtypeStruct((t // seq, seq // tm, N_HEADS * VT_ROWS, tm), jnp.bfloat16),
        jax.ShapeDtypeStruct((t, 2 * D_CONV + D_POOL), jnp.bfloat16),
    ]
    tile_t = lambda i: (i // (seq // tm), i % (seq // tm), 0, 0)
    return pl.pallas_call(
        functools.partial(_proj_kernel, tiles_per_seq=seq // tm),
        grid=(t // tm,),
        in_specs=[
            pl.BlockSpec((tm, D_MODEL), row),
            _const_spec((1, D_MODEL)),
            _const_spec((D_MODEL, A_COLS)),
            _const_spec((1, LANES)),
            _const_spec((1, D_ATTN)),
            _const_spec((1, D_ATTN)),
            _const_spec((D_ATTN, D_ATTN)),
            _const_spec((tm, tm)),
            _const_spec((N_PIECES * LANES, LANES)),
            _const_spec((N_PIECES * LANES, LANES)),
        ],
        out_specs=[
            pl.BlockSpec((1, 1, N_HEADS * LANES, tm), tile_t),
            pl.BlockSpec((tm, N_HEADS * LANES), row),
            pl.BlockSpec((1, 1, N_HEADS * VT_ROWS, tm), tile_t),
            pl.BlockSpec((tm, 2 * D_CONV + D_POOL), row),
        ],
        out_shape=outs,
        scratch_shapes=[pltpu.VMEM((1, LANES), jnp.float32)],
        compiler_params=pltpu.CompilerParams(dimension_semantics=("arbitrary",), vmem_limit_bytes=VMEM_LIMIT),
        name="proj",
    )(x2, g, w_a, fb, gq, gk, eh, tri, pq, pk)


def _attn_kernel(qt_ref, kh_ref, vt_ref, o_ref, acc_sc, aux_sc, *, online):
    i = pl.program_id(1)
    tq, tk = Q_TILE, K_TILE
    heads = range(N_HEADS)

    acc_sc[...] = jnp.zeros_like(acc_sc)
    m_sc = p_sc = aux_sc
    if online:
        m_sc[...] = jnp.full_like(m_sc, NEG_BIG)

    def logits(kt, hh, masked):
        start = pl.multiple_of(kt * tk, tk)
        s = _dot(kh_ref[0, pl.ds(start, tk), hh * LANES:(hh + 1) * LANES],
                 qt_ref[0, 0, hh * LANES:(hh + 1) * LANES, :])
        if masked:
            k_id = lax.broadcasted_iota(jnp.int32, s.shape, 0) + (kt - i) * tk
            q_id = lax.broadcasted_iota(jnp.int32, s.shape, 1)
            s = jnp.where(k_id <= q_id, s, NEG_BIG)
        return s

    def vt_tile(kt, hh):
        return vt_ref[0, kt, hh * VT_ROWS:(hh + 1) * VT_ROWS, :]

    if online:
        def step(kt, masked):
            for hh in heads:
                s = logits(kt, hh, masked)
                m_prev = m_sc[hh]
                m_new = jnp.maximum(m_prev, jnp.max(s, axis=0, keepdims=True))
                m_sc[hh] = m_new
                acc_sc[hh] = jnp.exp2(m_prev - m_new) * acc_sc[hh] + _dot(vt_tile(kt, hh), _bf16(jnp.exp2(s - m_new)))

        def body(kt, carry):
            step(kt, False)
            return carry

        lax.fori_loop(0, i, body, 0)
        step(i, True)
    else:
        def stage(kt, masked, rd, wr):
            for hh in heads:
                p_sc[wr, hh] = _bf16(jnp.exp2(logits(kt + 1, hh, masked)))
                acc_sc[hh] += _dot(vt_tile(kt, hh), p_sc[rd, hh])

        for hh in heads:
            p0 = _bf16(jnp.exp2(logits(0, hh, True)))
            p_sc[0, hh] = p0
            p_sc[1, hh] = p0

        n_plain = jnp.maximum(i - 1, 0)

        @pl.when(n_plain % 2 == 1)
        def _():
            stage(0, False, 0, 1)

        def body(j, carry):
            kt = n_plain % 2 + 2 * j
            stage(kt, False, 1, 0)
            stage(kt + 1, False, 0, 1)
            return carry

        lax.fori_loop(0, n_plain // 2, body, 0)

        @pl.when(i > 0)
        def _():
            stage(i - 1, True, 1, 0)

        for hh in heads:
            acc_sc[hh] += _dot(vt_tile(i, hh), p_sc[0, hh])

    for pr in range(N_HEADS // 2):
        o_t = []
        for hh in (2 * pr, 2 * pr + 1):
            acc = acc_sc[hh]
            o_t.append(acc[0:HEAD_DIM, :] / acc[HEAD_DIM:HEAD_DIM + 1, :])
        o_ref[0, :, pr * LANES:(pr + 1) * LANES] = _bf16(jnp.concatenate(o_t, axis=0).T)


def _attn_call(qt, kh, vt, *, online):
    b, s, _ = kh.shape
    tq = Q_TILE
    scratch = [pltpu.VMEM((N_HEADS, VT_ROWS, tq), jnp.float32),
               pltpu.VMEM((N_HEADS, 1, tq), jnp.float32) if online else pltpu.VMEM((2, N_HEADS, K_TILE, tq), jnp.bfloat16)]
    return pl.pallas_call(
        functools.partial(_attn_kernel, online=online),
        grid=(b, s // tq),
        in_specs=[
            pl.BlockSpec((1, 1, N_HEADS * LANES, tq), lambda bi, i: (bi, i, 0, 0)),
            pl.BlockSpec((1, s, N_HEADS * LANES), lambda bi, i: (bi, 0, 0)),
            pl.BlockSpec((1, s // K_TILE, N_HEADS * VT_ROWS, K_TILE), lambda bi, i: (bi, 0, 0, 0)),
        ],
        out_specs=pl.BlockSpec((1, tq, D_ATTN), lambda bi, i: (bi, i, 0)),
        out_shape=jax.ShapeDtypeStruct((b, s, D_ATTN), jnp.bfloat16),
        scratch_shapes=scratch,
        compiler_params=pltpu.CompilerParams(
            dimension_semantics=("arbitrary", "arbitrary"), vmem_limit_bytes=VMEM_LIMIT),
        name="attn_online" if online else "attn",
    )(qt, kh, vt)


def _shift_rows(x, k):
    return pltpu.roll(x, k, axis=0)


def _merge_kernel(x_ref, a_ref, cp_ref, halo_ref, g_ref, wg_ref, wao_ref, cw_ref, wco_ref, wpool_ref, ps_ref, wo_ref,
                  o_ref, *, tiles_per_seq):
    i = pl.program_id(0)
    tm = x_ref.shape[0]
    seq_tile = i % tiles_per_seq

    x = x_ref[...]
    h = _bf16(x * _rms_scale(x) * g_ref[...])

    halo = halo_ref[...].astype(jnp.float32) * jnp.where(seq_tile == 0, 0.0, 1.0)
    cp = cp_ref[...].astype(jnp.float32)
    z = jnp.concatenate([halo[:, 0:D_CONV], cp[:, 0:D_CONV]], axis=0)
    px = jnp.concatenate([halo[:, 2 * D_CONV:], cp[:, 2 * D_CONV:]], axis=0)
    cb = cp[:, D_CONV:2 * D_CONV]

    cw = cw_ref[...]
    conv = cw[2:3, :] * z + cw[1:2, :] * _shift_rows(z, 1) + cw[0:1, :] * _shift_rows(z, 2)
    u = _bf16(cb * conv[HALO:, :])
    y_conv = _dot(u, wco_ref[...])

    s2 = px + _shift_rows(px, 1)
    s4 = s2 + _shift_rows(s2, 2)
    s8 = s4 + _shift_rows(s4, 4)
    s16 = s8 + _shift_rows(s8, 8)
    grp = lax.broadcasted_iota(jnp.int32, (tm, D_POOL), 1) // POOL_GROUP_DIM
    wsum = jnp.where(grp == 0, s2[HALO:], jnp.where(grp == 1, s4[HALO:], jnp.where(grp == 2, s8[HALO:], s16[HALO:])))
    pos = seq_tile * tm + lax.broadcasted_iota(jnp.int32, (tm, D_POOL), 0)
    win = jnp.left_shift(2, grp)
    counts = jnp.minimum(pos + 1, win).astype(jnp.float32)
    d = _bf16(wsum / counts - px[HALO:])
    y_pool = _dot(d, wpool_ref[...]) * ps_ref[...]

    y_attn = _dot(a_ref[...], wao_ref[...])

    merged = jax.nn.sigmoid(_dot(h, wg_ref[:, 0:D_MODEL])) * y_attn
    merged += jax.nn.sigmoid(_dot(h, wg_ref[:, D_MODEL:2 * D_MODEL])) * y_conv
    merged += jax.nn.sigmoid(_dot(h, wg_ref[:, 2 * D_MODEL:3 * D_MODEL])) * y_pool
    o_ref[...] = x + _dot(_bf16(merged), wo_ref[...])


def _merge_call(x2, a2, cp, g, wg, wao, cw, wco, wpool, ps, wo, *, seq):
    t = x2.shape[0]
    tm = ROW_TILE
    row = lambda i: (i, 0)
    cpw = 2 * D_CONV + D_POOL
    halo_blocks = tm // HALO
    return pl.pallas_call(
        functools.partial(_merge_kernel, tiles_per_seq=seq // tm),
        grid=(t // tm,),
        in_specs=[
            pl.BlockSpec((tm, D_MODEL), row),
            pl.BlockSpec((tm, D_ATTN), row),
            pl.BlockSpec((tm, cpw), row),
            pl.BlockSpec((HALO, cpw), lambda i: (jnp.maximum(i * halo_blocks - 1, 0), 0)),
            _const_spec((1, D_MODEL)),
            _const_spec((D_MODEL, 3 * D_MODEL)),
            _const_spec((D_ATTN, D_MODEL)),
            _const_spec((8, D_CONV)),
            _const_spec((D_CONV, D_MODEL)),
            _const_spec((D_POOL, D_MODEL)),
            _const_spec((1, D_MODEL)),
            _const_spec((D_MODEL, D_MODEL)),
        ],
        out_specs=pl.BlockSpec((tm, D_MODEL), row),
        out_shape=jax.ShapeDtypeStruct((t, D_MODEL), jnp.float32),
        compiler_params=pltpu.CompilerParams(dimension_semantics=("arbitrary",), vmem_limit_bytes=VMEM_LIMIT),
        name="merge",
    )(x2, a2, cp, cp, g, wg, wao, cw, wco, wpool, ps, wo)


def _ffn_kernel(x_ref, g_ref, wi_ref, wo_ref, o_ref):
    x = x_ref[...]
    h = _bf16(x * _rms_scale(x) * g_ref[...])
    acc = x
    for c0, c1 in FF_CHUNKS:
        gt = _dot(h, wi_ref[:, c0:c1])
        up = _dot(h, wi_ref[:, D_FF + c0:D_FF + c1])
        act = _bf16(gt * jax.nn.sigmoid(gt) * up)
        acc = acc + _dot(act, wo_ref[c0:c1, :])
    o_ref[...] = acc


def _ffn_call(x2, g, wi, wo):
    t = x2.shape[0]
    tm = ROW_TILE
    row = lambda i: (i, 0)
    return pl.pallas_call(
        _ffn_kernel,
        grid=(t // tm,),
        in_specs=[
            pl.BlockSpec((tm, D_MODEL), row),
            _const_spec((1, D_MODEL)),
            _const_spec((D_MODEL, 2 * D_FF)),
            _const_spec((D_FF, D_MODEL)),
        ],
        out_specs=pl.BlockSpec((tm, D_MODEL), row),
        out_shape=jax.ShapeDtypeStruct((t, D_MODEL), jnp.float32),
        compiler_params=pltpu.CompilerParams(dimension_semantics=("arbitrary",), vmem_limit_bytes=VMEM_LIMIT),
        name="ffn",
    )(x2, g, wi, wo)


def _placement_matrices():
    pq = np.zeros((N_PIECES * LANES, LANES), np.float32)
    pk = np.zeros((N_PIECES * LANES, LANES), np.float32)
    ones_row = N_HEADS
    for h in range(N_HEADS):
        for j in range(N_PIECES):
            pq[j * LANES + h, N_PIECES * h + j] = 1.0
            pk[ones_row, N_PIECES * h + j] = 1.0
            pq[ones_row, X_ONES + N_PIECES * h + j] = 1.0
            pk[j * LANES + h, X_ONES + N_PIECES * h + j] = -1.0
    return jnp.asarray(pq, jnp.bfloat16), jnp.asarray(pk, jnp.bfloat16)


def kernel(x, norm_mix_g, w_in, forget_b, q_norm_g, k_norm_g, w_attn_out, conv_w, w_conv_out, pool_w, pool_scale,
           w_o, norm_ffn_g, w_ffn_in, w_ffn_out):
    bsz, seq, _ = x.shape
    depth = w_in.shape[0]
    assert seq % ROW_TILE == 0 and seq % Q_TILE == 0 and Q_TILE == K_TILE and ROW_TILE == K_TILE

    bf = jnp.bfloat16
    eh = jnp.asarray(np.kron(np.eye(N_HEADS), np.ones((HEAD_DIM, HEAD_DIM))), bf)
    tri = jnp.asarray(np.tril(np.ones((ROW_TILE, ROW_TILE))), bf)
    pq, pk = _placement_matrices()

    o_f = 3 * D_ATTN
    o_cx = o_f + N_HEADS
    o_g = o_cx + 3 * D_CONV + D_POOL
    w_qkv = w_in[:, :, :o_f]
    w_f = jnp.pad(w_in[:, :, o_f:o_cx], ((0, 0), (0, 0), (0, LANES - N_HEADS)))
    w_a = jnp.concatenate([w_qkv, w_in[:, :, o_cx:o_g], w_f], axis=2).astype(bf)
    w_g = w_in[:, :, o_g:].astype(bf)
    fb = jnp.pad(forget_b, ((0, 0), (0, LANES - N_HEADS)))[:, None, :]
    q_scale = (HEAD_DIM ** -0.5) * LOG2E
    gq = jnp.tile(q_norm_g, (1, N_HEADS))[:, None, :] * q_scale
    gk = jnp.tile(k_norm_g, (1, N_HEADS))[:, None, :]
    wao = w_attn_out.astype(bf)
    cw = jnp.pad(conv_w, ((0, 0), (0, 8 - CONV_K), (0, 0)))
    wco = w_conv_out.astype(bf)
    grp_mask = jnp.asarray(np.kron(np.eye(len(POOL_WINDOWS)), np.ones((POOL_GROUP_DIM, POOL_OUT_DIM))), jnp.float32)
    wpool = (jnp.tile(pool_w.reshape(depth, D_POOL, POOL_OUT_DIM), (1, 1, len(POOL_WINDOWS))) * grp_mask).astype(bf)
    wo = w_o.astype(bf)
    wfi = w_ffn_in.astype(bf)
    wfo = w_ffn_out.astype(bf)

    x2 = x.reshape(bsz * seq, D_MODEL)
    for l in range(depth):
        qt, kh, vt, cp = _proj_call(x2, norm_mix_g[l][None], w_a[l], fb[l], gq[l], gk[l], eh, tri, pq, pk, seq=seq)
        logit_bound = HEAD_DIM * jnp.max(jnp.abs(gq[l])) * jnp.max(jnp.abs(gk[l]))
        a = lax.cond(logit_bound < UNSHIFTED_LOGIT_LIMIT,
                     functools.partial(_attn_call, online=False), functools.partial(_attn_call, online=True),
                     qt, kh.reshape(bsz, seq, N_HEADS * LANES), vt)
        x2 = _merge_call(x2, a.reshape(bsz * seq, D_ATTN), cp, norm_mix_g[l][None], w_g[l], wao[l], cw[l], wco[l],
                         wpool[l], pool_scale[l][None], wo[l], seq=seq)
        x2 = _ffn_call(x2, norm_ffn_g[l][None], wfi[l], wfo[l])
    return x2.reshape(bsz, seq, D_MODEL)
```

```python
import functools
import math

import numpy as np
import jax
import jax.numpy as jnp
from jax import lax
from jax.experimental import pallas as pl
from jax.experimental.pallas import tpu as pltpu

D_MODEL = 1024
HEAD_DIM = 64
D_ATTN = D_MODEL // 2
N_HEADS = D_ATTN // HEAD_DIM
D_CONV = D_MODEL // 4
CONV_K = 3
D_POOL = D_MODEL // 4
POOL_WINDOWS = (2, 4, 8, 16)
POOL_GROUP_DIM = D_POOL // len(POOL_WINDOWS)
POOL_OUT_DIM = D_MODEL // len(POOL_WINDOWS)
D_FF = -(-8 * D_MODEL // (3 * 256)) * 256
EPS = 1e-6

LANES = 128
VT_ROWS = HEAD_DIM + 16
HALO = 16
N_PIECES = 3
LOG2E = 1.4426950408889634
NEG_BIG = -1e30
UNSHIFTED_LOGIT_LIMIT = 100.0

ROW_TILE = 512
Q_TILE = 512
K_TILE = 512
FF_CHUNKS = ((0, 1024), (1024, 2048), (2048, D_FF))
VMEM_LIMIT = 56 * 1024 * 1024

A_Q, A_K, A_V = 0, D_ATTN, 2 * D_ATTN
A_CX = 3 * D_ATTN
A_CB = A_CX + D_CONV
A_CC = A_CB + D_CONV
A_PX = A_CC + D_CONV
A_F = A_PX + D_POOL
A_COLS = A_F + LANES

X_ONES = N_HEADS * N_PIECES


def _bf16(x):
    return x.astype(jnp.bfloat16)


def _dot(a, b):
    return jnp.dot(a, b, preferred_element_type=jnp.float32)


def _split3(x):
    hi = _bf16(x)
    r1 = x - hi.astype(jnp.float32)
    mid = _bf16(r1)
    lo = _bf16(r1 - mid.astype(jnp.float32))
    return hi, mid, lo


def _rms_scale(x):
    return lax.rsqrt(jnp.mean(x * x, axis=-1, keepdims=True) + EPS)


def _proj_kernel(x_ref, g_ref, w_ref, fb_ref, gq_ref, gk_ref, eh_ref, tri_ref, pqk_ref,
                 qt_ref, kh_ref, vt_ref, cp_ref, carry_ref, *, tiles_per_seq):
    i = pl.program_id(0)

    @pl.when(i % tiles_per_seq == 0)
    def _():
        carry_ref[...] = jnp.zeros_like(carry_ref)

    x = x_ref[...]
    h = _bf16(x * _rms_scale(x) * g_ref[...])
    tm = x.shape[0]
    lane = lax.broadcasted_iota(jnp.int32, (tm, LANES), 1)

    f = _dot(h, w_ref[:, A_F:A_F + LANES]) + fb_ref[...]
    logf = jnp.minimum(f, 0.0) - jnp.log1p(jnp.exp(-jnp.abs(f)))
    logf = jnp.where(lane < N_HEADS, logf, 0.0)
    pieces = jnp.concatenate(_split3(logf)[:2], axis=1)
    cum = _dot(tri_ref[...], pieces)
    c = cum[:, 0:LANES] + cum[:, LANES:2 * LANES] + carry_ref[...]
    carry_ref[...] = c[tm - 1:, :]

    ap = jnp.concatenate(_split3(jnp.where(lane == N_HEADS, 1.0, c * LOG2E)), axis=1)
    ex = _dot(ap, pqk_ref[...])
    exq, exk = ex[:, 0:LANES], ex[:, LANES:2 * LANES]

    q = _dot(h, w_ref[:, A_Q:A_Q + D_ATTN])
    k = _dot(h, w_ref[:, A_K:A_K + D_ATTN])
    ssq = _dot(_bf16(q * q), eh_ref[...])
    ssk = _dot(_bf16(k * k), eh_ref[...])
    qn = q * lax.rsqrt(ssq * (1.0 / HEAD_DIM) + EPS) * gq_ref[...]
    kn = k * lax.rsqrt(ssk * (1.0 / HEAD_DIM) + EPS) * gk_ref[...]

    ex_keep = ((lane >= HEAD_DIM) & (lane < HEAD_DIM + N_PIECES)) \
        | ((lane >= HEAD_DIM + X_ONES) & (lane < HEAD_DIM + X_ONES + N_PIECES))
    for hd in range(N_HEADS):
        sl = slice((hd // 2) * LANES, (hd // 2 + 1) * LANES)
        blk_q, blk_k = qn[:, sl], kn[:, sl]
        if hd % 2:
            blk_q = pltpu.roll(blk_q, HEAD_DIM, axis=1)
            blk_k = pltpu.roll(blk_k, HEAD_DIM, axis=1)
        shift = HEAD_DIM - N_PIECES * hd
        qh = jnp.where(lane < HEAD_DIM, blk_q, jnp.where(ex_keep, pltpu.roll(exq, shift, axis=1), 0.0))
        kh = jnp.where(lane < HEAD_DIM, blk_k, jnp.where(ex_keep, pltpu.roll(exk, shift, axis=1), 0.0))
        qt_ref[0, 0, hd * LANES:(hd + 1) * LANES, :] = _bf16(qh.T)
        kh_ref[:, hd * LANES:(hd + 1) * LANES] = _bf16(kh)

    v = _dot(h, w_ref[:, A_V:A_V + D_ATTN])
    ones_rows = _bf16(jnp.where(lax.broadcasted_iota(jnp.int32, (VT_ROWS - HEAD_DIM, tm), 0) == 0, 1.0, 0.0))
    for pr in range(N_HEADS // 2):
        vt = _bf16(v[:, pr * LANES:(pr + 1) * LANES].T)
        for hh in range(2):
            base = (2 * pr + hh) * VT_ROWS
            vt_ref[0, 0, base:base + HEAD_DIM, :] = vt[hh * HEAD_DIM:(hh + 1) * HEAD_DIM, :]
            vt_ref[0, 0, base + HEAD_DIM:base + VT_ROWS, :] = ones_rows

    r = _dot(h, w_ref[:, A_CX:A_F])
    cp_ref[:, 0:D_CONV] = _bf16(r[:, A_CC - A_CX:A_CC - A_CX + D_CONV] * r[:, 0:D_CONV])
    cp_ref[:, D_CONV:2 * D_CONV] = _bf16(r[:, A_CB - A_CX:A_CB - A_CX + D_CONV])
    cp_ref[:, 2 * D_CONV:2 * D_CONV + D_POOL] = _bf16(r[:, A_PX - A_CX:A_PX - A_CX + D_POOL])


def _const_spec(shape):
    return pl.BlockSpec(shape, lambda *_: (0,) * len(shape), pipeline_mode=pl.Buffered(1))


def _proj_call(x2, g, w_a, fb, gq, gk, eh, tri, pqk, *, seq):
    t = x2.shape[0]
    tm = ROW_TILE
    row = lambda i: (i, 0)
    outs = [
        jax.ShapeDtypeStruct((t // seq, seq // tm, N_HEADS * LANES, tm), jnp.bfloat16),
        jax.ShapeDtypeStruct((t, N_HEADS * LANES), jnp.bfloat16),
        jax.ShapeDtypeStruct((t // seq, seq // tm, N_HEADS * VT_ROWS, tm), jnp.bfloat16),
        jax.ShapeDtypeStruct((t, 2 * D_CONV + D_POOL), jnp.bfloat16),
    ]
    tile_t = lambda i: (i // (seq // tm), i % (seq // tm), 0, 0)
    return pl.pallas_call(
        functools.partial(_proj_kernel, tiles_per_seq=seq // tm),
        grid=(t // tm,),
        in_specs=[
            pl.BlockSpec((tm, D_MODEL), row),
            _const_spec((1, D_MODEL)),
            _const_spec((D_MODEL, A_COLS)),
            _const_spec((1, LANES)),
            _const_spec((1, D_ATTN)),
            _const_spec((1, D_ATTN)),
            _const_spec((D_ATTN, D_ATTN)),
            _const_spec((tm, tm)),
            _const_spec((N_PIECES * LANES, 2 * LANES)),
        ],
        out_specs=[
            pl.BlockSpec((1, 1, N_HEADS * LANES, tm), tile_t),
            pl.BlockSpec((tm, N_HEADS * LANES), row),
            pl.BlockSpec((1, 1, N_HEADS * VT_ROWS, tm), tile_t),
            pl.BlockSpec((tm, 2 * D_CONV + D_POOL), row),
        ],
        out_shape=outs,
        scratch_shapes=[pltpu.VMEM((1, LANES), jnp.float32)],
        compiler_params=pltpu.CompilerParams(dimension_semantics=("arbitrary",), vmem_limit_bytes=VMEM_LIMIT),
        name="proj",
    )(x2, g, w_a, fb, gq, gk, eh, tri, pqk)


def _attn_kernel(qt_ref, kh_ref, vt_ref, o_ref, acc_sc, aux_sc, *, online):
    i = pl.program_id(1)
    tq, tk = Q_TILE, K_TILE
    heads = range(N_HEADS)

    acc_sc[...] = jnp.zeros_like(acc_sc)
    m_sc = p_sc = aux_sc
    if online:
        m_sc[...] = jnp.full_like(m_sc, NEG_BIG)

    def logits(kt, hh, masked):
        start = pl.multiple_of(kt * tk, tk)
        s = _dot(kh_ref[0, pl.ds(start, tk), hh * LANES:(hh + 1) * LANES],
                 qt_ref[0, 0, hh * LANES:(hh + 1) * LANES, :])
        if masked:
            k_id = lax.broadcasted_iota(jnp.int32, s.shape, 0) + (kt - i) * tk
            q_id = lax.broadcasted_iota(jnp.int32, s.shape, 1)
            s = jnp.where(k_id <= q_id, s, NEG_BIG)
        return s

    def vt_tile(kt, hh):
        return vt_ref[0, kt, hh * VT_ROWS:(hh + 1) * VT_ROWS, :]

    if online:
        def step(kt, masked):
            for hh in heads:
                s = logits(kt, hh, masked)
                m_prev = m_sc[hh]
                m_new = jnp.maximum(m_prev, jnp.max(s, axis=0, keepdims=True))
                m_sc[hh] = m_new
                acc_sc[hh] = jnp.exp2(m_prev - m_new) * acc_sc[hh] + _dot(vt_tile(kt, hh), _bf16(jnp.exp2(s - m_new)))

        def body(kt, carry):
            step(kt, False)
            return carry

        lax.fori_loop(0, i, body, 0)
        step(i, True)
    else:
        def stage(kt, masked, rd, wr):
            for hh in heads:
                p_sc[wr, hh] = _bf16(jnp.exp2(logits(kt + 1, hh, masked)))
                acc_sc[hh] += _dot(vt_tile(kt, hh), p_sc[rd, hh])

        for hh in heads:
            p0 = _bf16(jnp.exp2(logits(0, hh, True)))
            p_sc[0, hh] = p0
            p_sc[1, hh] = p0

        n_plain = jnp.maximum(i - 1, 0)

        @pl.when(n_plain % 2 == 1)
        def _():
            stage(0, False, 0, 1)

        def body(j, carry):
            kt = n_plain % 2 + 2 * j
            stage(kt, False, 1, 0)
            stage(kt + 1, False, 0, 1)
            return carry

        lax.fori_loop(0, n_plain // 2, body, 0)

        @pl.when(i > 0)
        def _():
            stage(i - 1, True, 1, 0)

        for hh in heads:
            acc_sc[hh] += _dot(vt_tile(i, hh), p_sc[0, hh])

    for pr in range(N_HEADS // 2):
        o_t = []
        for hh in (2 * pr, 2 * pr + 1):
            acc = acc_sc[hh]
            o_t.append(acc[0:HEAD_DIM, :] / acc[HEAD_DIM:HEAD_DIM + 1, :])
        o_ref[0, :, pr * LANES:(pr + 1) * LANES] = _bf16(jnp.concatenate(o_t, axis=0).T)


def _attn_call(qt, kh, vt, *, online):
    b, s, _ = kh.shape
    tq = Q_TILE
    scratch = [pltpu.VMEM((N_HEADS, VT_ROWS, tq), jnp.float32),
               pltpu.VMEM((N_HEADS, 1, tq), jnp.float32) if online else pltpu.VMEM((2, N_HEADS, K_TILE, tq), jnp.bfloat16)]
    return pl.pallas_call(
        functools.partial(_attn_kernel, online=online),
        grid=(b, s // tq),
        in_specs=[
            pl.BlockSpec((1, 1, N_HEADS * LANES, tq), lambda bi, i: (bi, i, 0, 0)),
            pl.BlockSpec((1, s, N_HEADS * LANES), lambda bi, i: (bi, 0, 0)),
            pl.BlockSpec((1, s // K_TILE, N_HEADS * VT_ROWS, K_TILE), lambda bi, i: (bi, 0, 0, 0)),
        ],
        out_specs=pl.BlockSpec((1, tq, D_ATTN), lambda bi, i: (bi, i, 0)),
        out_shape=jax.ShapeDtypeStruct((b, s, D_ATTN), jnp.bfloat16),
        scratch_shapes=scratch,
        compiler_params=pltpu.CompilerParams(
            dimension_semantics=("arbitrary", "arbitrary"), vmem_limit_bytes=VMEM_LIMIT),
        name="attn_online" if online else "attn",
    )(qt, kh, vt)


def _shift_rows(x, k):
    return pltpu.roll(x, k, axis=0)


def _merge_rows(x, a_ref, cp_ref, halo_ref, g_ref, wg_ref, wao_ref, cw_ref, wco_ref, wpool_ref, ps_ref, wo_ref,
                seq_tile):
    tm = x.shape[0]
    h = _bf16(x * _rms_scale(x) * g_ref[...])

    halo = halo_ref[...].astype(jnp.float32) * jnp.where(seq_tile == 0, 0.0, 1.0)
    cp = cp_ref[...].astype(jnp.float32)
    z = jnp.concatenate([halo[:, 0:D_CONV], cp[:, 0:D_CONV]], axis=0)
    px = jnp.concatenate([halo[:, 2 * D_CONV:], cp[:, 2 * D_CONV:]], axis=0)
    cb = cp[:, D_CONV:2 * D_CONV]

    cw = cw_ref[...]
    conv = cw[2:3, :] * z + cw[1:2, :] * _shift_rows(z, 1) + cw[0:1, :] * _shift_rows(z, 2)
    u = _bf16(cb * conv[HALO:, :])
    y_conv = _dot(u, wco_ref[...])

    s2 = px + _shift_rows(px, 1)
    s4 = s2 + _shift_rows(s2, 2)
    s8 = s4 + _shift_rows(s4, 4)
    s16 = s8 + _shift_rows(s8, 8)
    grp = lax.broadcasted_iota(jnp.int32, (tm, D_POOL), 1) // POOL_GROUP_DIM
    wsum = jnp.where(grp == 0, s2[HALO:], jnp.where(grp == 1, s4[HALO:], jnp.where(grp == 2, s8[HALO:], s16[HALO:])))
    pos = seq_tile * tm + lax.broadcasted_iota(jnp.int32, (tm, D_POOL), 0)
    win = jnp.left_shift(2, grp)
    counts = jnp.minimum(pos + 1, win).astype(jnp.float32)
    d = _bf16(wsum / counts - px[HALO:])
    y_pool = _dot(d, wpool_ref[...]) * ps_ref[...]

    y_attn = _dot(a_ref[...], wao_ref[...])

    merged = jax.nn.sigmoid(_dot(h, wg_ref[:, 0:D_MODEL])) * y_attn
    merged += jax.nn.sigmoid(_dot(h, wg_ref[:, D_MODEL:2 * D_MODEL])) * y_conv
    merged += jax.nn.sigmoid(_dot(h, wg_ref[:, 2 * D_MODEL:3 * D_MODEL])) * y_pool
    return x + _dot(_bf16(merged), wo_ref[...])


def _ffn_rows(x, g_ref, wi_ref, wo_ref):
    h = _bf16(x * _rms_scale(x) * g_ref[...])
    acc = x
    for c0, c1 in FF_CHUNKS:
        gt = _dot(h, wi_ref[:, c0:c1])
        up = _dot(h, wi_ref[:, D_FF + c0:D_FF + c1])
        act = _bf16(gt * jax.nn.sigmoid(gt) * up)
        acc = acc + _dot(act, wo_ref[c0:c1, :])
    return acc


def _mix_ffn_kernel(x_ref, a_ref, cp_ref, halo_ref, g_ref, wg_ref, wao_ref, cw_ref, wco_ref, wpool_ref, ps_ref, wo_ref,
                    g2_ref, wi_ref, wfo_ref, o_ref, *, tiles_per_seq):
    seq_tile = pl.program_id(0) % tiles_per_seq
    x1 = _merge_rows(x_ref[...], a_ref, cp_ref, halo_ref, g_ref, wg_ref, wao_ref, cw_ref, wco_ref, wpool_ref, ps_ref,
                     wo_ref, seq_tile)
    o_ref[...] = _ffn_rows(x1, g2_ref, wi_ref, wfo_ref)


def _mix_ffn_call(x2, a2, cp, g, wg, wao, cw, wco, wpool, ps, wo, g2, wi, wfo, *, seq):
    t = x2.shape[0]
    tm = ROW_TILE
    row = lambda i: (i, 0)
    cpw = 2 * D_CONV + D_POOL
    halo_blocks = tm // HALO
    return pl.pallas_call(
        functools.partial(_mix_ffn_kernel, tiles_per_seq=seq // tm),
        grid=(t // tm,),
        in_specs=[
            pl.BlockSpec((tm, D_MODEL), row),
            pl.BlockSpec((tm, D_ATTN), row),
            pl.BlockSpec((tm, cpw), row),
            pl.BlockSpec((HALO, cpw), lambda i: (jnp.maximum(i * halo_blocks - 1, 0), 0)),
            _const_spec((1, D_MODEL)),
            _const_spec((D_MODEL, 3 * D_MODEL)),
            _const_spec((D_ATTN, D_MODEL)),
            _const_spec((8, D_CONV)),
            _const_spec((D_CONV, D_MODEL)),
            _const_spec((D_POOL, D_MODEL)),
            _const_spec((1, D_MODEL)),
            _const_spec((D_MODEL, D_MODEL)),
            _const_spec((1, D_MODEL)),
            _const_spec((D_MODEL, 2 * D_FF)),
            _const_spec((D_FF, D_MODEL)),
        ],
        out_specs=pl.BlockSpec((tm, D_MODEL), row),
        out_shape=jax.ShapeDtypeStruct((t, D_MODEL), jnp.float32),
        compiler_params=pltpu.CompilerParams(dimension_semantics=("arbitrary",), vmem_limit_bytes=VMEM_LIMIT),
        name="mix_ffn",
    )(x2, a2, cp, cp, g, wg, wao, cw, wco, wpool, ps, wo, g2, wi, wfo)


def _placement_matrices():
    pq = np.zeros((N_PIECES * LANES, LANES), np.float32)
    pk = np.zeros((N_PIECES * LANES, LANES), np.float32)
    ones_row = N_HEADS
    for h in range(N_HEADS):
        for j in range(N_PIECES):
            pq[j * LANES + h, N_PIECES * h + j] = 1.0
            pk[ones_row, N_PIECES * h + j] = 1.0
            pq[ones_row, X_ONES + N_PIECES * h + j] = 1.0
            pk[j * LANES + h, X_ONES + N_PIECES * h + j] = -1.0
    return jnp.asarray(np.concatenate([pq, pk], axis=1), jnp.bfloat16)


def kernel(x, norm_mix_g, w_in, forget_b, q_norm_g, k_norm_g, w_attn_out, conv_w, w_conv_out, pool_w, pool_scale,
           w_o, norm_ffn_g, w_ffn_in, w_ffn_out):
    bsz, seq, _ = x.shape
    depth = w_in.shape[0]
    assert seq % ROW_TILE == 0 and seq % Q_TILE == 0 and Q_TILE == K_TILE and ROW_TILE == K_TILE

    bf = jnp.bfloat16
    eh = jnp.asarray(np.kron(np.eye(N_HEADS), np.ones((HEAD_DIM, HEAD_DIM))), bf)
    tri = jnp.asarray(np.tril(np.ones((ROW_TILE, ROW_TILE))), bf)
    pqk = _placement_matrices()

    o_f = 3 * D_ATTN
    o_cx = o_f + N_HEADS
    o_g = o_cx + 3 * D_CONV + D_POOL
    w_qkv = w_in[:, :, :o_f]
    w_f = jnp.pad(w_in[:, :, o_f:o_cx], ((0, 0), (0, 0), (0, LANES - N_HEADS)))
    w_a = jnp.concatenate([w_qkv, w_in[:, :, o_cx:o_g], w_f], axis=2).astype(bf)
    w_g = w_in[:, :, o_g:].astype(bf)
    fb = jnp.pad(forget_b, ((0, 0), (0, LANES - N_HEADS)))[:, None, :]
    q_scale = (HEAD_DIM ** -0.5) * LOG2E
    gq = jnp.tile(q_norm_g, (1, N_HEADS))[:, None, :] * q_scale
    gk = jnp.tile(k_norm_g, (1, N_HEADS))[:, None, :]
    wao = w_attn_out.astype(bf)
    cw = jnp.pad(conv_w, ((0, 0), (0, 8 - CONV_K), (0, 0)))
    wco = w_conv_out.astype(bf)
    grp_mask = jnp.asarray(np.kron(np.eye(len(POOL_WINDOWS)), np.ones((POOL_GROUP_DIM, POOL_OUT_DIM))), jnp.float32)
    wpool = (jnp.tile(pool_w.reshape(depth, D_POOL, POOL_OUT_DIM), (1, 1, len(POOL_WINDOWS))) * grp_mask).astype(bf)
    wo = w_o.astype(bf)
    wfi = w_ffn_in.astype(bf)
    wfo = w_ffn_out.astype(bf)

    x2 = x.reshape(bsz * seq, D_MODEL)
    for l in range(depth):
        qt, kh, vt, cp = _proj_call(x2, norm_mix_g[l][None], w_a[l], fb[l], gq[l], gk[l], eh, tri, pqk, seq=seq)
        logit_bound = HEAD_DIM * jnp.max(jnp.abs(gq[l])) * jnp.max(jnp.abs(gk[l]))
        a = lax.cond(logit_bound < UNSHIFTED_LOGIT_LIMIT,
                     functools.partial(_attn_call, online=False), functools.partial(_attn_call, online=True),
                     qt, kh.reshape(bsz, seq, N_HEADS * LANES), vt)
        x2 = _mix_ffn_call(x2, a.reshape(bsz * seq, D_ATTN), cp, norm_mix_g[l][None], w_g[l], wao[l], cw[l], wco[l],
                           wpool[l], pool_scale[l][None], wo[l], norm_ffn_g[l][None], wfi[l], wfo[l], seq=seq)
    return x2.reshape(bsz, seq, D_MODEL)
```

```python
import functools
import math

import numpy as np
import jax
import jax.numpy as jnp
from jax import lax
from jax.experimental import pallas as pl
from jax.experimental.pallas import tpu as pltpu

D_MODEL = 1024
HEAD_DIM = 64
D_ATTN = D_MODEL // 2
N_HEADS = D_ATTN // HEAD_DIM
D_CONV = D_MODEL // 4
CONV_K = 3
D_POOL = D_MODEL // 4
POOL_WINDOWS = (2, 4, 8, 16)
POOL_GROUP_DIM = D_POOL // len(POOL_WINDOWS)
POOL_OUT_DIM = D_MODEL // len(POOL_WINDOWS)
D_FF = -(-8 * D_MODEL // (3 * 256)) * 256
EPS = 1e-6

LANES = 128
VT_ROWS = HEAD_DIM + 16
HALO = 16
N_PIECES = 3
LOG2E = 1.4426950408889634
NEG_BIG = -1e30
UNSHIFTED_LOGIT_LIMIT = 100.0

ROW_TILE = 512
Q_TILE = 512
K_TILE = 512
FF_CHUNKS = ((0, 1024), (1024, 2048), (2048, D_FF))
VMEM_LIMIT = 56 * 1024 * 1024

IN_F = 3 * D_ATTN
IN_C = IN_F + N_HEADS
IN_G = IN_C + 3 * D_CONV + D_POOL
C_CX, C_CB, C_CC, C_PX = 0, D_CONV, 2 * D_CONV, 3 * D_CONV

X_ONES = N_HEADS * N_PIECES


def _bf16(x):
    return x.astype(jnp.bfloat16)


def _dot(a, b):
    return jnp.dot(a, b, preferred_element_type=jnp.float32)


def _split3(x):
    hi = _bf16(x)
    r1 = x - hi.astype(jnp.float32)
    mid = _bf16(r1)
    lo = _bf16(r1 - mid.astype(jnp.float32))
    return hi, mid, lo


def _rms_scale(x):
    return lax.rsqrt(jnp.mean(x * x, axis=-1, keepdims=True) + EPS)


def _proj_kernel(x_ref, g_ref, wqkv_ref, wf_ref, wc_ref, fb_ref, gq_ref, gk_ref, eh_ref, tri_ref, pqk_ref,
                 qt_ref, kh_ref, vt_ref, cp_ref, carry_ref, *, tiles_per_seq):
    i = pl.program_id(0)

    @pl.when(i % tiles_per_seq == 0)
    def _():
        carry_ref[...] = jnp.zeros_like(carry_ref)

    x = x_ref[...]
    h = _bf16(x * _rms_scale(x) * g_ref[...])
    tm = x.shape[0]
    lane = lax.broadcasted_iota(jnp.int32, (tm, LANES), 1)

    f = _dot(h, wf_ref[...]) + fb_ref[...]
    logf = jnp.minimum(f, 0.0) - jnp.log1p(jnp.exp(-jnp.abs(f)))
    logf = jnp.where(lane < N_HEADS, logf, 0.0)
    pieces = jnp.concatenate(_split3(logf)[:2], axis=1)
    cum = _dot(tri_ref[...], pieces)
    c = cum[:, 0:LANES] + cum[:, LANES:2 * LANES] + carry_ref[...]
    carry_ref[...] = c[tm - 1:, :]

    ap = jnp.concatenate(_split3(jnp.where(lane == N_HEADS, 1.0, c * LOG2E)), axis=1)
    ex = _dot(ap, pqk_ref[...])
    exq, exk = ex[:, 0:LANES], ex[:, LANES:2 * LANES]

    q = _dot(h, wqkv_ref[:, 0:D_ATTN])
    k = _dot(h, wqkv_ref[:, D_ATTN:2 * D_ATTN])
    ssq = _dot(_bf16(q * q), eh_ref[...])
    ssk = _dot(_bf16(k * k), eh_ref[...])
    qn = q * lax.rsqrt(ssq * (1.0 / HEAD_DIM) + EPS) * gq_ref[...]
    kn = k * lax.rsqrt(ssk * (1.0 / HEAD_DIM) + EPS) * gk_ref[...]

    ex_keep = ((lane >= HEAD_DIM) & (lane < HEAD_DIM + N_PIECES)) \
        | ((lane >= HEAD_DIM + X_ONES) & (lane < HEAD_DIM + X_ONES + N_PIECES))
    for hd in range(N_HEADS):
        sl = slice((hd // 2) * LANES, (hd // 2 + 1) * LANES)
        blk_q, blk_k = qn[:, sl], kn[:, sl]
        if hd % 2:
            blk_q = pltpu.roll(blk_q, HEAD_DIM, axis=1)
            blk_k = pltpu.roll(blk_k, HEAD_DIM, axis=1)
        shift = HEAD_DIM - N_PIECES * hd
        qh = jnp.where(lane < HEAD_DIM, blk_q, jnp.where(ex_keep, pltpu.roll(exq, shift, axis=1), 0.0))
        kh = jnp.where(lane < HEAD_DIM, blk_k, jnp.where(ex_keep, pltpu.roll(exk, shift, axis=1), 0.0))
        qt_ref[0, 0, hd * LANES:(hd + 1) * LANES, :] = _bf16(qh.T)
        kh_ref[:, hd * LANES:(hd + 1) * LANES] = _bf16(kh)

    v = _dot(h, wqkv_ref[:, 2 * D_ATTN:3 * D_ATTN])
    ones_rows = _bf16(jnp.where(lax.broadcasted_iota(jnp.int32, (VT_ROWS - HEAD_DIM, tm), 0) == 0, 1.0, 0.0))
    for pr in range(N_HEADS // 2):
        vt = _bf16(v[:, pr * LANES:(pr + 1) * LANES].T)
        for hh in range(2):
            base = (2 * pr + hh) * VT_ROWS
            vt_ref[0, 0, base:base + HEAD_DIM, :] = vt[hh * HEAD_DIM:(hh + 1) * HEAD_DIM, :]
            vt_ref[0, 0, base + HEAD_DIM:base + VT_ROWS, :] = ones_rows

    r = _dot(h, wc_ref[...])
    cp_ref[:, 0:D_CONV] = _bf16(r[:, C_CC:C_CC + D_CONV] * r[:, C_CX:C_CX + D_CONV])
    cp_ref[:, D_CONV:2 * D_CONV] = _bf16(r[:, C_CB:C_CB + D_CONV])
    cp_ref[:, 2 * D_CONV:2 * D_CONV + D_POOL] = _bf16(r[:, C_PX:C_PX + D_POOL])


def _const_spec(shape):
    return pl.BlockSpec(shape, lambda *_: (0,) * len(shape), pipeline_mode=pl.Buffered(1))


def _layer_spec(layer, shape):
    return pl.BlockSpec((None,) + shape, lambda *_: (layer,) + (0,) * len(shape), pipeline_mode=pl.Buffered(1))


def _proj_call(layer, x2, g, wqkv, wf, wc, fb, gq, gk, eh, tri, pqk, *, seq):
    t = x2.shape[0]
    tm = ROW_TILE
    row = lambda i: (i, 0)
    outs = [
        jax.ShapeDtypeStruct((t // seq, seq // tm, N_HEADS * LANES, tm), jnp.bfloat16),
        jax.ShapeDtypeStruct((t, N_HEADS * LANES), jnp.bfloat16),
        jax.ShapeDtypeStruct((t // seq, seq // tm, N_HEADS * VT_ROWS, tm), jnp.bfloat16),
        jax.ShapeDtypeStruct((t, 2 * D_CONV + D_POOL), jnp.bfloat16),
    ]
    tile_t = lambda i: (i // (seq // tm), i % (seq // tm), 0, 0)
    return pl.pallas_call(
        functools.partial(_proj_kernel, tiles_per_seq=seq // tm),
        grid=(t // tm,),
        in_specs=[
            pl.BlockSpec((tm, D_MODEL), row),
            _layer_spec(layer, (1, D_MODEL)),
            _layer_spec(layer, (D_MODEL, 3 * D_ATTN)),
            _layer_spec(layer, (D_MODEL, LANES)),
            _layer_spec(layer, (D_MODEL, 3 * D_CONV + D_POOL)),
            _layer_spec(layer, (1, LANES)),
            _layer_spec(layer, (1, D_ATTN)),
            _layer_spec(layer, (1, D_ATTN)),
            _const_spec((D_ATTN, D_ATTN)),
            _const_spec((tm, tm)),
            _const_spec((N_PIECES * LANES, 2 * LANES)),
        ],
        out_specs=[
            pl.BlockSpec((1, 1, N_HEADS * LANES, tm), tile_t),
            pl.BlockSpec((tm, N_HEADS * LANES), row),
            pl.BlockSpec((1, 1, N_HEADS * VT_ROWS, tm), tile_t),
            pl.BlockSpec((tm, 2 * D_CONV + D_POOL), row),
        ],
        out_shape=outs,
        scratch_shapes=[pltpu.VMEM((1, LANES), jnp.float32)],
        compiler_params=pltpu.CompilerParams(dimension_semantics=("arbitrary",), vmem_limit_bytes=VMEM_LIMIT),
        name="proj",
    )(x2, g, wqkv, wf, wc, fb, gq, gk, eh, tri, pqk)


def _attn_kernel(qt_ref, kh_ref, vt_ref, o_ref, acc_sc, aux_sc, *, online):
    i = pl.program_id(1)
    tq, tk = Q_TILE, K_TILE
    heads = range(N_HEADS)

    acc_sc[...] = jnp.zeros_like(acc_sc)
    m_sc = p_sc = aux_sc
    if online:
        m_sc[...] = jnp.full_like(m_sc, NEG_BIG)

    def logits(kt, hh, masked):
        start = pl.multiple_of(kt * tk, tk)
        s = _dot(kh_ref[0, pl.ds(start, tk), hh * LANES:(hh + 1) * LANES],
                 qt_ref[0, 0, hh * LANES:(hh + 1) * LANES, :])
        if masked:
            k_id = lax.broadcasted_iota(jnp.int32, s.shape, 0) + (kt - i) * tk
            q_id = lax.broadcasted_iota(jnp.int32, s.shape, 1)
            s = jnp.where(k_id <= q_id, s, NEG_BIG)
        return s

    def vt_tile(kt, hh):
        return vt_ref[0, kt, hh * VT_ROWS:(hh + 1) * VT_ROWS, :]

    if online:
        def step(kt, masked):
            for hh in heads:
                s = logits(kt, hh, masked)
                m_prev = m_sc[hh]
                m_new = jnp.maximum(m_prev, jnp.max(s, axis=0, keepdims=True))
                m_sc[hh] = m_new
                acc_sc[hh] = jnp.exp2(m_prev - m_new) * acc_sc[hh] + _dot(vt_tile(kt, hh), _bf16(jnp.exp2(s - m_new)))

        def body(kt, carry):
            step(kt, False)
            return carry

        lax.fori_loop(0, i, body, 0)
        step(i, True)
    else:
        def stage(kt, masked, rd, wr):
            for hh in heads:
                p_sc[wr, hh] = _bf16(jnp.exp2(logits(kt + 1, hh, masked)))
                acc_sc[hh] += _dot(vt_tile(kt, hh), p_sc[rd, hh])

        for hh in heads:
            p0 = _bf16(jnp.exp2(logits(0, hh, True)))
            p_sc[0, hh] = p0
            p_sc[1, hh] = p0

        n_plain = jnp.maximum(i - 1, 0)

        @pl.when(n_plain % 2 == 1)
        def _():
            stage(0, False, 0, 1)

        def body(j, carry):
            kt = n_plain % 2 + 2 * j
            stage(kt, False, 1, 0)
            stage(kt + 1, False, 0, 1)
            return carry

        lax.fori_loop(0, n_plain // 2, body, 0)

        @pl.when(i > 0)
        def _():
            stage(i - 1, True, 1, 0)

        for hh in heads:
            acc_sc[hh] += _dot(vt_tile(i, hh), p_sc[0, hh])

    for pr in range(N_HEADS // 2):
        o_t = []
        for hh in (2 * pr, 2 * pr + 1):
            acc = acc_sc[hh]
            o_t.append(acc[0:HEAD_DIM, :] / acc[HEAD_DIM:HEAD_DIM + 1, :])
        o_ref[0, :, pr * LANES:(pr + 1) * LANES] = _bf16(jnp.concatenate(o_t, axis=0).T)


def _attn_call(qt, kh, vt, *, online):
    b, s, _ = kh.shape
    tq = Q_TILE
    scratch = [pltpu.VMEM((N_HEADS, VT_ROWS, tq), jnp.float32),
               pltpu.VMEM((N_HEADS, 1, tq), jnp.float32) if online else pltpu.VMEM((2, N_HEADS, K_TILE, tq), jnp.bfloat16)]
    return pl.pallas_call(
        functools.partial(_attn_kernel, online=online),
        grid=(b, s // tq),
        in_specs=[
            pl.BlockSpec((1, 1, N_HEADS * LANES, tq), lambda bi, i: (bi, i, 0, 0)),
            pl.BlockSpec((1, s, N_HEADS * LANES), lambda bi, i: (bi, 0, 0)),
            pl.BlockSpec((1, s // K_TILE, N_HEADS * VT_ROWS, K_TILE), lambda bi, i: (bi, 0, 0, 0)),
        ],
        out_specs=pl.BlockSpec((1, tq, D_ATTN), lambda bi, i: (bi, i, 0)),
        out_shape=jax.ShapeDtypeStruct((b, s, D_ATTN), jnp.bfloat16),
        scratch_shapes=scratch,
        compiler_params=pltpu.CompilerParams(
            dimension_semantics=("arbitrary", "arbitrary"), vmem_limit_bytes=VMEM_LIMIT),
        name="attn_online" if online else "attn",
    )(qt, kh, vt)


def _shift_rows(x, k):
    return pltpu.roll(x, k, axis=0)


def _merge_rows(x, a_ref, cp_ref, halo_ref, g_ref, wg_ref, wao_ref, cw_ref, wco_ref, wpool_ref, ps_ref, wo_ref,
                seq_tile):
    tm = x.shape[0]
    h = _bf16(x * _rms_scale(x) * g_ref[...])

    halo = halo_ref[...].astype(jnp.float32) * jnp.where(seq_tile == 0, 0.0, 1.0)
    cp = cp_ref[...].astype(jnp.float32)
    z = jnp.concatenate([halo[:, 0:D_CONV], cp[:, 0:D_CONV]], axis=0)
    px = jnp.concatenate([halo[:, 2 * D_CONV:], cp[:, 2 * D_CONV:]], axis=0)
    cb = cp[:, D_CONV:2 * D_CONV]

    cw = cw_ref[...]
    conv = cw[2:3, :] * z + cw[1:2, :] * _shift_rows(z, 1) + cw[0:1, :] * _shift_rows(z, 2)
    u = _bf16(cb * conv[HALO:, :])
    y_conv = _dot(u, wco_ref[...])

    s2 = px + _shift_rows(px, 1)
    s4 = s2 + _shift_rows(s2, 2)
    s8 = s4 + _shift_rows(s4, 4)
    s16 = s8 + _shift_rows(s8, 8)
    grp = lax.broadcasted_iota(jnp.int32, (tm, D_POOL), 1) // POOL_GROUP_DIM
    wsum = jnp.where(grp == 0, s2[HALO:], jnp.where(grp == 1, s4[HALO:], jnp.where(grp == 2, s8[HALO:], s16[HALO:])))
    pos = seq_tile * tm + lax.broadcasted_iota(jnp.int32, (tm, D_POOL), 0)
    win = jnp.left_shift(2, grp)
    counts = jnp.minimum(pos + 1, win).astype(jnp.float32)
    d = _bf16(wsum / counts - px[HALO:])
    y_pool = _dot(d, wpool_ref[...]) * ps_ref[...]

    y_attn = _dot(a_ref[...], wao_ref[...])

    merged = jax.nn.sigmoid(_dot(h, wg_ref[:, 0:D_MODEL])) * y_attn
    merged += jax.nn.sigmoid(_dot(h, wg_ref[:, D_MODEL:2 * D_MODEL])) * y_conv
    merged += jax.nn.sigmoid(_dot(h, wg_ref[:, 2 * D_MODEL:3 * D_MODEL])) * y_pool
    return x + _dot(_bf16(merged), wo_ref[...])


def _ffn_rows(x, g_ref, wi_ref, wo_ref):
    h = _bf16(x * _rms_scale(x) * g_ref[...])
    acc = x
    for c0, c1 in FF_CHUNKS:
        gt = _dot(h, wi_ref[:, c0:c1])
        up = _dot(h, wi_ref[:, D_FF + c0:D_FF + c1])
        act = _bf16(gt * jax.nn.sigmoid(gt) * up)
        acc = acc + _dot(act, wo_ref[c0:c1, :])
    return acc


def _mix_ffn_kernel(x_ref, a_ref, cp_ref, halo_ref, g_ref, wg_ref, wao_ref, cw_ref, wco_ref, wpool_ref, ps_ref, wo_ref,
                    g2_ref, wi_ref, wfo_ref, o_ref, *, tiles_per_seq):
    seq_tile = pl.program_id(0) % tiles_per_seq
    x1 = _merge_rows(x_ref[...], a_ref, cp_ref, halo_ref, g_ref, wg_ref, wao_ref, cw_ref, wco_ref, wpool_ref, ps_ref,
                     wo_ref, seq_tile)
    o_ref[...] = _ffn_rows(x1, g2_ref, wi_ref, wfo_ref)


def _mix_ffn_call(layer, x2, a2, cp, g, wg, wao, cw, wco, wpool, ps, wo, g2, wi, wfo, *, seq):
    t = x2.shape[0]
    tm = ROW_TILE
    row = lambda i: (i, 0)
    cpw = 2 * D_CONV + D_POOL
    halo_blocks = tm // HALO
    return pl.pallas_call(
        functools.partial(_mix_ffn_kernel, tiles_per_seq=seq // tm),
        grid=(t // tm,),
        in_specs=[
            pl.BlockSpec((tm, D_MODEL), row),
            pl.BlockSpec((tm, D_ATTN), row),
            pl.BlockSpec((tm, cpw), row),
            pl.BlockSpec((HALO, cpw), lambda i: (jnp.maximum(i * halo_blocks - 1, 0), 0)),
            _layer_spec(layer, (1, D_MODEL)),
            _layer_spec(layer, (D_MODEL, 3 * D_MODEL)),
            _layer_spec(layer, (D_ATTN, D_MODEL)),
            _layer_spec(layer, (8, D_CONV)),
            _layer_spec(layer, (D_CONV, D_MODEL)),
            _layer_spec(layer, (D_POOL, D_MODEL)),
            _layer_spec(layer, (1, D_MODEL)),
            _layer_spec(layer, (D_MODEL, D_MODEL)),
            _layer_spec(layer, (1, D_MODEL)),
            _layer_spec(layer, (D_MODEL, 2 * D_FF)),
            _layer_spec(layer, (D_FF, D_MODEL)),
        ],
        out_specs=pl.BlockSpec((tm, D_MODEL), row),
        out_shape=jax.ShapeDtypeStruct((t, D_MODEL), jnp.float32),
        compiler_params=pltpu.CompilerParams(dimension_semantics=("arbitrary",), vmem_limit_bytes=VMEM_LIMIT),
        name="mix_ffn",
    )(x2, a2, cp, cp, g, wg, wao, cw, wco, wpool, ps, wo, g2, wi, wfo)


def _placement_matrices():
    pq = np.zeros((N_PIECES * LANES, LANES), np.float32)
    pk = np.zeros((N_PIECES * LANES, LANES), np.float32)
    ones_row = N_HEADS
    for h in range(N_HEADS):
        for j in range(N_PIECES):
            pq[j * LANES + h, N_PIECES * h + j] = 1.0
            pk[ones_row, N_PIECES * h + j] = 1.0
            pq[ones_row, X_ONES + N_PIECES * h + j] = 1.0
            pk[j * LANES + h, X_ONES + N_PIECES * h + j] = -1.0
    return jnp.asarray(np.concatenate([pq, pk], axis=1), jnp.bfloat16)


def kernel(x, norm_mix_g, w_in, forget_b, q_norm_g, k_norm_g, w_attn_out, conv_w, w_conv_out, pool_w, pool_scale,
           w_o, norm_ffn_g, w_ffn_in, w_ffn_out):
    bsz, seq, _ = x.shape
    depth = w_in.shape[0]
    assert seq % ROW_TILE == 0 and seq % Q_TILE == 0 and Q_TILE == K_TILE and ROW_TILE == K_TILE

    bf = jnp.bfloat16
    eh = jnp.asarray(np.kron(np.eye(N_HEADS), np.ones((HEAD_DIM, HEAD_DIM))), bf)
    tri = jnp.asarray(np.tril(np.ones((ROW_TILE, ROW_TILE))), bf)
    pqk = _placement_matrices()

    row3 = lambda p: p[:, None, :]
    w_qkv = w_in[:, :, :IN_F].astype(bf)
    w_f = jnp.pad(w_in[:, :, IN_F:IN_C], ((0, 0), (0, 0), (0, LANES - N_HEADS))).astype(bf)
    w_c = w_in[:, :, IN_C:IN_G].astype(bf)
    w_g = w_in[:, :, IN_G:].astype(bf)
    fb = row3(jnp.pad(forget_b, ((0, 0), (0, LANES - N_HEADS))))
    q_scale = (HEAD_DIM ** -0.5) * LOG2E
    gq = row3(jnp.tile(q_norm_g, (1, N_HEADS))) * q_scale
    gk = row3(jnp.tile(k_norm_g, (1, N_HEADS)))
    g_mix, g_ffn, p_scale = row3(norm_mix_g), row3(norm_ffn_g), row3(pool_scale)
    wao = w_attn_out.astype(bf)
    cw = jnp.pad(conv_w, ((0, 0), (0, 8 - CONV_K), (0, 0)))
    wco = w_conv_out.astype(bf)
    grp_mask = jnp.asarray(np.kron(np.eye(len(POOL_WINDOWS)), np.ones((POOL_GROUP_DIM, POOL_OUT_DIM))), jnp.float32)
    wpool = (jnp.tile(pool_w.reshape(depth, D_POOL, POOL_OUT_DIM), (1, 1, len(POOL_WINDOWS))) * grp_mask).astype(bf)
    wo = w_o.astype(bf)
    wfi = w_ffn_in.astype(bf)
    wfo = w_ffn_out.astype(bf)

    x2 = x.reshape(bsz * seq, D_MODEL)
    for l in range(depth):
        qt, kh, vt, cp = _proj_call(l, x2, g_mix, w_qkv, w_f, w_c, fb, gq, gk, eh, tri, pqk, seq=seq)
        logit_bound = HEAD_DIM * jnp.max(jnp.abs(gq[l])) * jnp.max(jnp.abs(gk[l]))
        a = lax.cond(logit_bound < UNSHIFTED_LOGIT_LIMIT,
                     functools.partial(_attn_call, online=False), functools.partial(_attn_call, online=True),
                     qt, kh.reshape(bsz, seq, N_HEADS * LANES), vt)
        x2 = _mix_ffn_call(l, x2, a.reshape(bsz * seq, D_ATTN), cp, g_mix, w_g, wao, cw, wco, wpool, p_scale, wo,
                           g_ffn, wfi, wfo, seq=seq)
    return x2.reshape(bsz, seq, D_MODEL)
```

```python
import functools
import math

import numpy as np
import jax
import jax.numpy as jnp
from jax import lax
from jax.experimental import pallas as pl
from jax.experimental.pallas import tpu as pltpu

D_MODEL = 1024
HEAD_DIM = 64
D_ATTN = D_MODEL // 2
N_HEADS = D_ATTN // HEAD_DIM
D_CONV = D_MODEL // 4
CONV_K = 3
D_POOL = D_MODEL // 4
POOL_WINDOWS = (2, 4, 8, 16)
POOL_GROUP_DIM = D_POOL // len(POOL_WINDOWS)
POOL_OUT_DIM = D_MODEL // len(POOL_WINDOWS)
D_FF = -(-8 * D_MODEL // (3 * 256)) * 256
EPS = 1e-6

LANES = 128
VT_ROWS = HEAD_DIM + 16
HALO = 16
N_PIECES = 3
LOG2E = 1.4426950408889634
NEG_BIG = -1e30
UNSHIFTED_LOGIT_LIMIT = 100.0

ROW_TILE = 512
SUB_ROWS = 256
Q_TILE = 512
K_TILE = 512
FF_CHUNKS = ((0, 1024), (1024, 2048), (2048, D_FF))
VMEM_LIMIT = 56 * 1024 * 1024

IN_F = 3 * D_ATTN
IN_C = IN_F + N_HEADS
IN_G = IN_C + 3 * D_CONV + D_POOL
C_CX, C_CB, C_CC, C_PX = 0, D_CONV, 2 * D_CONV, 3 * D_CONV

X_ONES = N_HEADS * N_PIECES


def _bf16(x):
    return x.astype(jnp.bfloat16)


def _dot(a, b):
    return jnp.dot(a, b, preferred_element_type=jnp.float32)


def _split3(x):
    hi = _bf16(x)
    r1 = x - hi.astype(jnp.float32)
    mid = _bf16(r1)
    lo = _bf16(r1 - mid.astype(jnp.float32))
    return hi, mid, lo


def _rms_scale(x):
    return lax.rsqrt(jnp.mean(x * x, axis=-1, keepdims=True) + EPS)


def _proj_kernel(x_ref, g_ref, wqkv_ref, wf_ref, wc_ref, fb_ref, gq_ref, gk_ref, eh_ref, tri_ref, pqk_ref,
                 qt_ref, kh_ref, vt_ref, cp_ref, carry_ref, *, tiles_per_seq):
    i = pl.program_id(0)

    @pl.when(i % tiles_per_seq == 0)
    def _():
        carry_ref[...] = jnp.zeros_like(carry_ref)

    x = x_ref[...]
    h = _bf16(x * _rms_scale(x) * g_ref[...])
    tm = x.shape[0]
    lane = lax.broadcasted_iota(jnp.int32, (tm, LANES), 1)

    f = _dot(h, wf_ref[...]) + fb_ref[...]
    logf = jnp.minimum(f, 0.0) - jnp.log1p(jnp.exp(-jnp.abs(f)))
    logf = jnp.where(lane < N_HEADS, logf, 0.0)
    pieces = jnp.concatenate(_split3(logf)[:2], axis=1)
    cum = _dot(tri_ref[...], pieces)
    c = cum[:, 0:LANES] + cum[:, LANES:2 * LANES] + carry_ref[...]
    carry_ref[...] = c[tm - 1:, :]

    ap = jnp.concatenate(_split3(jnp.where(lane == N_HEADS, 1.0, c * LOG2E)), axis=1)
    ex = _dot(ap, pqk_ref[...])
    exq, exk = ex[:, 0:LANES], ex[:, LANES:2 * LANES]

    q = _dot(h, wqkv_ref[:, 0:D_ATTN])
    k = _dot(h, wqkv_ref[:, D_ATTN:2 * D_ATTN])
    ssq = _dot(_bf16(q * q), eh_ref[...])
    ssk = _dot(_bf16(k * k), eh_ref[...])
    qn = q * lax.rsqrt(ssq * (1.0 / HEAD_DIM) + EPS) * gq_ref[...]
    kn = k * lax.rsqrt(ssk * (1.0 / HEAD_DIM) + EPS) * gk_ref[...]

    ex_keep = ((lane >= HEAD_DIM) & (lane < HEAD_DIM + N_PIECES)) \
        | ((lane >= HEAD_DIM + X_ONES) & (lane < HEAD_DIM + X_ONES + N_PIECES))
    for hd in range(N_HEADS):
        sl = slice((hd // 2) * LANES, (hd // 2 + 1) * LANES)
        blk_q, blk_k = qn[:, sl], kn[:, sl]
        if hd % 2:
            blk_q = pltpu.roll(blk_q, HEAD_DIM, axis=1)
            blk_k = pltpu.roll(blk_k, HEAD_DIM, axis=1)
        shift = HEAD_DIM - N_PIECES * hd
        qh = jnp.where(lane < HEAD_DIM, blk_q, jnp.where(ex_keep, pltpu.roll(exq, shift, axis=1), 0.0))
        kh = jnp.where(lane < HEAD_DIM, blk_k, jnp.where(ex_keep, pltpu.roll(exk, shift, axis=1), 0.0))
        qt_ref[0, 0, hd * LANES:(hd + 1) * LANES, :] = _bf16(qh.T)
        kh_ref[:, hd * LANES:(hd + 1) * LANES] = _bf16(kh)

    v = _dot(h, wqkv_ref[:, 2 * D_ATTN:3 * D_ATTN])
    ones_rows = _bf16(jnp.where(lax.broadcasted_iota(jnp.int32, (VT_ROWS - HEAD_DIM, tm), 0) == 0, 1.0, 0.0))
    for pr in range(N_HEADS // 2):
        vt = _bf16(v[:, pr * LANES:(pr + 1) * LANES].T)
        for hh in range(2):
            base = (2 * pr + hh) * VT_ROWS
            vt_ref[0, 0, base:base + HEAD_DIM, :] = vt[hh * HEAD_DIM:(hh + 1) * HEAD_DIM, :]
            vt_ref[0, 0, base + HEAD_DIM:base + VT_ROWS, :] = ones_rows

    r = _dot(h, wc_ref[...])
    cp_ref[:, 0:D_CONV] = _bf16(r[:, C_CC:C_CC + D_CONV] * r[:, C_CX:C_CX + D_CONV])
    cp_ref[:, D_CONV:2 * D_CONV] = _bf16(r[:, C_CB:C_CB + D_CONV])
    cp_ref[:, 2 * D_CONV:2 * D_CONV + D_POOL] = _bf16(r[:, C_PX:C_PX + D_POOL])


def _const_spec(shape):
    return pl.BlockSpec(shape, lambda *_: (0,) * len(shape), pipeline_mode=pl.Buffered(1))


def _layer_spec(layer, shape):
    return pl.BlockSpec((None,) + shape, lambda *_: (layer,) + (0,) * len(shape), pipeline_mode=pl.Buffered(1))


def _proj_call(layer, x2, g, wqkv, wf, wc, fb, gq, gk, eh, tri, pqk, *, seq):
    t = x2.shape[0]
    tm = ROW_TILE
    row = lambda i: (i, 0)
    outs = [
        jax.ShapeDtypeStruct((t // seq, seq // tm, N_HEADS * LANES, tm), jnp.bfloat16),
        jax.ShapeDtypeStruct((t, N_HEADS * LANES), jnp.bfloat16),
        jax.ShapeDtypeStruct((t // seq, seq // tm, N_HEADS * VT_ROWS, tm), jnp.bfloat16),
        jax.ShapeDtypeStruct((t, 2 * D_CONV + D_POOL), jnp.bfloat16),
    ]
    tile_t = lambda i: (i // (seq // tm), i % (seq // tm), 0, 0)
    return pl.pallas_call(
        functools.partial(_proj_kernel, tiles_per_seq=seq // tm),
        grid=(t // tm,),
        in_specs=[
            pl.BlockSpec((tm, D_MODEL), row),
            _layer_spec(layer, (1, D_MODEL)),
            _layer_spec(layer, (D_MODEL, 3 * D_ATTN)),
            _layer_spec(layer, (D_MODEL, LANES)),
            _layer_spec(layer, (D_MODEL, 3 * D_CONV + D_POOL)),
            _layer_spec(layer, (1, LANES)),
            _layer_spec(layer, (1, D_ATTN)),
            _layer_spec(layer, (1, D_ATTN)),
            _const_spec((D_ATTN, D_ATTN)),
            _const_spec((tm, tm)),
            _const_spec((N_PIECES * LANES, 2 * LANES)),
        ],
        out_specs=[
            pl.BlockSpec((1, 1, N_HEADS * LANES, tm), tile_t),
            pl.BlockSpec((tm, N_HEADS * LANES), row),
            pl.BlockSpec((1, 1, N_HEADS * VT_ROWS, tm), tile_t),
            pl.BlockSpec((tm, 2 * D_CONV + D_POOL), row),
        ],
        out_shape=outs,
        scratch_shapes=[pltpu.VMEM((1, LANES), jnp.float32)],
        compiler_params=pltpu.CompilerParams(dimension_semantics=("arbitrary",), vmem_limit_bytes=VMEM_LIMIT),
        name="proj",
    )(x2, g, wqkv, wf, wc, fb, gq, gk, eh, tri, pqk)


def _attn_kernel(qt_ref, kh_ref, vt_ref, o_ref, acc_sc, aux_sc, *, online):
    i = pl.program_id(1)
    tq, tk = Q_TILE, K_TILE
    heads = range(N_HEADS)

    acc_sc[...] = jnp.zeros_like(acc_sc)
    m_sc = p_sc = aux_sc
    if online:
        m_sc[...] = jnp.full_like(m_sc, NEG_BIG)

    def logits(kt, hh, masked, k0=0, k1=tk, q0=0):
        start = pl.multiple_of(kt * tk + k0, math.gcd(tk, k0))
        s = _dot(kh_ref[0, pl.ds(start, k1 - k0), hh * LANES:(hh + 1) * LANES],
                 qt_ref[0, 0, hh * LANES:(hh + 1) * LANES, q0:])
        if masked:
            k_id = lax.broadcasted_iota(jnp.int32, s.shape, 0) + (kt - i) * tk + k0
            q_id = lax.broadcasted_iota(jnp.int32, s.shape, 1) + q0
            s = jnp.where(k_id <= q_id, s, NEG_BIG)
        return s

    def vt_tile(kt, hh):
        return vt_ref[0, kt, hh * VT_ROWS:(hh + 1) * VT_ROWS, :]

    if online:
        def step(kt, masked):
            for hh in heads:
                s = logits(kt, hh, masked)
                m_prev = m_sc[hh]
                m_new = jnp.maximum(m_prev, jnp.max(s, axis=0, keepdims=True))
                m_sc[hh] = m_new
                acc_sc[hh] = jnp.exp2(m_prev - m_new) * acc_sc[hh] + _dot(vt_tile(kt, hh), _bf16(jnp.exp2(s - m_new)))

        def body(kt, carry):
            step(kt, False)
            return carry

        lax.fori_loop(0, i, body, 0)
        step(i, True)
    else:
        def stage(kt, rd, wr):
            for hh in heads:
                p_sc[wr, hh] = _bf16(jnp.exp2(logits(kt + 1, hh, False)))
                acc_sc[hh] += _dot(vt_tile(kt, hh), p_sc[rd, hh])

        for hh in heads:
            p0 = _bf16(jnp.exp2(logits(0, hh, True)))
            p_sc[0, hh] = p0
            p_sc[1, hh] = p0

        n_plain = jnp.maximum(i - 1, 0)

        @pl.when(n_plain % 2 == 1)
        def _():
            stage(0, 0, 1)

        def body(j, carry):
            kt = n_plain % 2 + 2 * j
            stage(kt, 1, 0)
            stage(kt + 1, 0, 1)
            return carry

        lax.fori_loop(0, n_plain // 2, body, 0)

        half = tk // 2

        @pl.when(i > 0)
        def _():
            for hh in heads:
                p_sc[0, hh, 0:half, :] = _bf16(jnp.exp2(logits(i, hh, True, 0, half)))
                p_sc[0, hh, half:tk, half:tq] = _bf16(jnp.exp2(logits(i, hh, True, half, tk, half)))
                acc_sc[hh] += _dot(vt_tile(i - 1, hh), p_sc[1, hh])

        for hh in heads:
            vt = vt_tile(i, hh)
            early = _dot(vt[:, 0:half], p_sc[0, hh, 0:half, :])
            late = _dot(vt[:, half:tk], p_sc[0, hh, half:tk, half:tq])
            acc_sc[hh, :, 0:half] += early[:, 0:half]
            acc_sc[hh, :, half:tq] += early[:, half:tq] + late

    for pr in range(N_HEADS // 2):
        o_t = []
        for hh in (2 * pr, 2 * pr + 1):
            acc = acc_sc[hh]
            o_t.append(acc[0:HEAD_DIM, :] / acc[HEAD_DIM:HEAD_DIM + 1, :])
        o_ref[0, :, pr * LANES:(pr + 1) * LANES] = _bf16(jnp.concatenate(o_t, axis=0).T)


def _attn_call(qt, kh, vt, *, online):
    b, s, _ = kh.shape
    tq = Q_TILE
    scratch = [pltpu.VMEM((N_HEADS, VT_ROWS, tq), jnp.float32),
               pltpu.VMEM((N_HEADS, 1, tq), jnp.float32) if online else pltpu.VMEM((2, N_HEADS, K_TILE, tq), jnp.bfloat16)]
    return pl.pallas_call(
        functools.partial(_attn_kernel, online=online),
        grid=(b, s // tq),
        in_specs=[
            pl.BlockSpec((1, 1, N_HEADS * LANES, tq), lambda bi, i: (bi, i, 0, 0)),
            pl.BlockSpec((1, s, N_HEADS * LANES), lambda bi, i: (bi, 0, 0)),
            pl.BlockSpec((1, s // K_TILE, N_HEADS * VT_ROWS, K_TILE), lambda bi, i: (bi, 0, 0, 0)),
        ],
        out_specs=pl.BlockSpec((1, tq, D_ATTN), lambda bi, i: (bi, i, 0)),
        out_shape=jax.ShapeDtypeStruct((b, s, D_ATTN), jnp.bfloat16),
        scratch_shapes=scratch,
        compiler_params=pltpu.CompilerParams(
            dimension_semantics=("arbitrary", "arbitrary"), vmem_limit_bytes=VMEM_LIMIT),
        name="attn_online" if online else "attn",
    )(qt, kh, vt)


def _shift_rows(x, k):
    return pltpu.roll(x, k, axis=0)


def _branch_inputs(cp_ref, halo_ref, cw_ref, seq_tile):
    tm = cp_ref.shape[0]
    halo = halo_ref[...].astype(jnp.float32) * jnp.where(seq_tile == 0, 0.0, 1.0)
    cp = cp_ref[...].astype(jnp.float32)
    z = jnp.concatenate([halo[:, 0:D_CONV], cp[:, 0:D_CONV]], axis=0)
    px = jnp.concatenate([halo[:, 2 * D_CONV:], cp[:, 2 * D_CONV:]], axis=0)
    cb = cp[:, D_CONV:2 * D_CONV]

    cw = cw_ref[...]
    conv = cw[2:3, :] * z + cw[1:2, :] * _shift_rows(z, 1) + cw[0:1, :] * _shift_rows(z, 2)
    u = _bf16(cb * conv[HALO:, :])

    s2 = px + _shift_rows(px, 1)
    s4 = s2 + _shift_rows(s2, 2)
    s8 = s4 + _shift_rows(s4, 4)
    s16 = s8 + _shift_rows(s8, 8)
    grp = lax.broadcasted_iota(jnp.int32, (tm, D_POOL), 1) // POOL_GROUP_DIM
    wsum = jnp.where(grp == 0, s2[HALO:], jnp.where(grp == 1, s4[HALO:], jnp.where(grp == 2, s8[HALO:], s16[HALO:])))
    pos = seq_tile * tm + lax.broadcasted_iota(jnp.int32, (tm, D_POOL), 0)
    win = jnp.left_shift(2, grp)
    counts = jnp.minimum(pos + 1, win).astype(jnp.float32)
    d = _bf16(wsum / counts - px[HALO:])
    return u, d


def _merge_rows(x, a, u, d, g_ref, wg_ref, wao_ref, wco_ref, wpool_ref, ps_ref, wo_ref):
    h = _bf16(x * _rms_scale(x) * g_ref[...])
    y_attn = _dot(a, wao_ref[...])
    y_conv = _dot(u, wco_ref[...])
    y_pool = _dot(d, wpool_ref[...]) * ps_ref[...]
    merged = jax.nn.sigmoid(_dot(h, wg_ref[:, 0:D_MODEL])) * y_attn
    merged += jax.nn.sigmoid(_dot(h, wg_ref[:, D_MODEL:2 * D_MODEL])) * y_conv
    merged += jax.nn.sigmoid(_dot(h, wg_ref[:, 2 * D_MODEL:3 * D_MODEL])) * y_pool
    return x + _dot(_bf16(merged), wo_ref[...])


def _ffn_rows(x, g_ref, wi_ref, wo_ref):
    h = _bf16(x * _rms_scale(x) * g_ref[...])
    acc = x
    for c0, c1 in FF_CHUNKS:
        gt = _dot(h, wi_ref[:, c0:c1])
        up = _dot(h, wi_ref[:, D_FF + c0:D_FF + c1])
        act = _bf16(gt * jax.nn.sigmoid(gt) * up)
        acc = acc + _dot(act, wo_ref[c0:c1, :])
    return acc


def _mix_ffn_kernel(x_ref, a_ref, cp_ref, halo_ref, g_ref, wg_ref, wao_ref, cw_ref, wco_ref, wpool_ref, ps_ref, wo_ref,
                    g2_ref, wi_ref, wfo_ref, o_ref, *, tiles_per_seq):
    seq_tile = pl.program_id(0) % tiles_per_seq
    u, d = _branch_inputs(cp_ref, halo_ref, cw_ref, seq_tile)
    for r0 in range(0, x_ref.shape[0], SUB_ROWS):
        rows = slice(r0, r0 + SUB_ROWS)
        x1 = _merge_rows(x_ref[rows, :], a_ref[rows, :], u[rows], d[rows], g_ref, wg_ref, wao_ref, wco_ref, wpool_ref,
                         ps_ref, wo_ref)
        o_ref[rows, :] = _ffn_rows(x1, g2_ref, wi_ref, wfo_ref)


def _mix_ffn_call(layer, x2, a2, cp, g, wg, wao, cw, wco, wpool, ps, wo, g2, wi, wfo, *, seq):
    t = x2.shape[0]
    tm = ROW_TILE
    row = lambda i: (i, 0)
    cpw = 2 * D_CONV + D_POOL
    halo_blocks = tm // HALO
    return pl.pallas_call(
        functools.partial(_mix_ffn_kernel, tiles_per_seq=seq // tm),
        grid=(t // tm,),
        in_specs=[
            pl.BlockSpec((tm, D_MODEL), row),
            pl.BlockSpec((tm, D_ATTN), row),
            pl.BlockSpec((tm, cpw), row),
            pl.BlockSpec((HALO, cpw), lambda i: (jnp.maximum(i * halo_blocks - 1, 0), 0)),
            _layer_spec(layer, (1, D_MODEL)),
            _layer_spec(layer, (D_MODEL, 3 * D_MODEL)),
            _layer_spec(layer, (D_ATTN, D_MODEL)),
            _layer_spec(layer, (8, D_CONV)),
            _layer_spec(layer, (D_CONV, D_MODEL)),
            _layer_spec(layer, (D_POOL, D_MODEL)),
            _layer_spec(layer, (1, D_MODEL)),
            _layer_spec(layer, (D_MODEL, D_MODEL)),
            _layer_spec(layer, (1, D_MODEL)),
            _layer_spec(layer, (D_MODEL, 2 * D_FF)),
            _layer_spec(layer, (D_FF, D_MODEL)),
        ],
        out_specs=pl.BlockSpec((tm, D_MODEL), row),
        out_shape=jax.ShapeDtypeStruct((t, D_MODEL), jnp.float32),
        compiler_params=pltpu.CompilerParams(dimension_semantics=("arbitrary",), vmem_limit_bytes=VMEM_LIMIT),
        name="mix_ffn",
    )(x2, a2, cp, cp, g, wg, wao, cw, wco, wpool, ps, wo, g2, wi, wfo)


def _placement_matrices():
    pq = np.zeros((N_PIECES * LANES, LANES), np.float32)
    pk = np.zeros((N_PIECES * LANES, LANES), np.float32)
    ones_row = N_HEADS
    for h in range(N_HEADS):
        for j in range(N_PIECES):
            pq[j * LANES + h, N_PIECES * h + j] = 1.0
            pk[ones_row, N_PIECES * h + j] = 1.0
            pq[ones_row, X_ONES + N_PIECES * h + j] = 1.0
            pk[j * LANES + h, X_ONES + N_PIECES * h + j] = -1.0
    return jnp.asarray(np.concatenate([pq, pk], axis=1), jnp.bfloat16)


def kernel(x, norm_mix_g, w_in, forget_b, q_norm_g, k_norm_g, w_attn_out, conv_w, w_conv_out, pool_w, pool_scale,
           w_o, norm_ffn_g, w_ffn_in, w_ffn_out):
    bsz, seq, _ = x.shape
    depth = w_in.shape[0]
    assert seq % ROW_TILE == 0 and seq % Q_TILE == 0 and Q_TILE == K_TILE and ROW_TILE == K_TILE

    bf = jnp.bfloat16
    eh = jnp.asarray(np.kron(np.eye(N_HEADS), np.ones((HEAD_DIM, HEAD_DIM))), bf)
    tri = jnp.asarray(np.tril(np.ones((ROW_TILE, ROW_TILE))), bf)
    pqk = _placement_matrices()

    row3 = lambda p: p[:, None, :]
    w_qkv = w_in[:, :, :IN_F].astype(bf)
    w_f = jnp.pad(w_in[:, :, IN_F:IN_C], ((0, 0), (0, 0), (0, LANES - N_HEADS))).astype(bf)
    w_c = w_in[:, :, IN_C:IN_G].astype(bf)
    w_g = w_in[:, :, IN_G:].astype(bf)
    fb = row3(jnp.pad(forget_b, ((0, 0), (0, LANES - N_HEADS))))
    q_scale = (HEAD_DIM ** -0.5) * LOG2E
    gq = row3(jnp.tile(q_norm_g, (1, N_HEADS))) * q_scale
    gk = row3(jnp.tile(k_norm_g, (1, N_HEADS)))
    g_mix, g_ffn, p_scale = row3(norm_mix_g), row3(norm_ffn_g), row3(pool_scale)
    wao = w_attn_out.astype(bf)
    cw = jnp.pad(conv_w, ((0, 0), (0, 8 - CONV_K), (0, 0)))
    wco = w_conv_out.astype(bf)
    grp_mask = jnp.asarray(np.kron(np.eye(len(POOL_WINDOWS)), np.ones((POOL_GROUP_DIM, POOL_OUT_DIM))), jnp.float32)
    wpool = (jnp.tile(pool_w.reshape(depth, D_POOL, POOL_OUT_DIM), (1, 1, len(POOL_WINDOWS))) * grp_mask).astype(bf)
    wo = w_o.astype(bf)
    wfi = w_ffn_in.astype(bf)
    wfo = w_ffn_out.astype(bf)

    x2 = x.reshape(bsz * seq, D_MODEL)
    for l in range(depth):
        qt, kh, vt, cp = _proj_call(l, x2, g_mix, w_qkv, w_f, w_c, fb, gq, gk, eh, tri, pqk, seq=seq)
        logit_bound = HEAD_DIM * jnp.max(jnp.abs(gq[l])) * jnp.max(jnp.abs(gk[l]))
        a = lax.cond(logit_bound < UNSHIFTED_LOGIT_LIMIT,
                     functools.partial(_attn_call, online=False), functools.partial(_attn_call, online=True),
                     qt, kh.reshape(bsz, seq, N_HEADS * LANES), vt)
        x2 = _mix_ffn_call(l, x2, a.reshape(bsz * seq, D_ATTN), cp, g_mix, w_g, wao, cw, wco, wpool, p_scale, wo,
                           g_ffn, wfi, wfo, seq=seq)
    return x2.reshape(bsz, seq, D_MODEL)
```

```python
import functools
import math

import numpy as np
import jax
import jax.numpy as jnp
from jax import lax
from jax.experimental import pallas as pl
from jax.experimental.pallas import tpu as pltpu

D_MODEL = 1024
HEAD_DIM = 64
D_ATTN = D_MODEL // 2
N_HEADS = D_ATTN // HEAD_DIM
D_CONV = D_MODEL // 4
CONV_K = 3
D_POOL = D_MODEL // 4
POOL_WINDOWS = (2, 4, 8, 16)
POOL_GROUP_DIM = D_POOL // len(POOL_WINDOWS)
POOL_OUT_DIM = D_MODEL // len(POOL_WINDOWS)
D_FF = -(-8 * D_MODEL // (3 * 256)) * 256
EPS = 1e-6

LANES = 128
VT_ROWS = HEAD_DIM + 16
HALO = 16
N_PIECES = 3
LOG2E = 1.4426950408889634
NEG_BIG = -1e30
UNSHIFTED_LOGIT_LIMIT = 100.0

ROW_TILE = 512
W_IN_ROWS = 256
SUB_ROWS = 256
Q_TILE = 512
K_TILE = 512
FF_CHUNKS = ((0, 1024), (1024, 2048), (2048, D_FF))
VMEM_LIMIT = 56 * 1024 * 1024

IN_F = 3 * D_ATTN
IN_C = IN_F + N_HEADS
IN_G = IN_C + 3 * D_CONV + D_POOL
C_CX, C_CB, C_CC, C_PX = 0, D_CONV, 2 * D_CONV, 3 * D_CONV

X_ONES = N_HEADS * N_PIECES


def _bf16(x):
    return x.astype(jnp.bfloat16)


def _dot(a, b):
    return jnp.dot(a, b, preferred_element_type=jnp.float32)


def _split3(x):
    hi = _bf16(x)
    r1 = x - hi.astype(jnp.float32)
    mid = _bf16(r1)
    lo = _bf16(r1 - mid.astype(jnp.float32))
    return hi, mid, lo


def _rms_scale(x):
    return lax.rsqrt(jnp.mean(x * x, axis=-1, keepdims=True) + EPS)


def _proj_kernel(x_ref, g_ref, wqkv_ref, wf_ref, wc_ref, fb_ref, gq_ref, gk_ref, eh_ref, tri_ref, pqk_ref,
                 qt_ref, kh_ref, vt_ref, cp_ref, carry_ref, *, tiles_per_seq):
    i = pl.program_id(0)

    @pl.when(i % tiles_per_seq == 0)
    def _():
        carry_ref[...] = jnp.zeros_like(carry_ref)

    sub = SUB_ROWS
    lane = lax.broadcasted_iota(jnp.int32, (sub, LANES), 1)
    ex_keep = ((lane >= HEAD_DIM) & (lane < HEAD_DIM + N_PIECES)) \
        | ((lane >= HEAD_DIM + X_ONES) & (lane < HEAD_DIM + X_ONES + N_PIECES))
    ones_rows = _bf16(jnp.where(lax.broadcasted_iota(jnp.int32, (VT_ROWS - HEAD_DIM, sub), 0) == 0, 1.0, 0.0))
    carry = carry_ref[...]

    for r0 in range(0, x_ref.shape[0], sub):
        rows = slice(r0, r0 + sub)
        x = x_ref[rows, :]
        h = _bf16(x * _rms_scale(x) * g_ref[...])

        f = _dot(h, wf_ref[...]) + fb_ref[...]
        logf = jnp.minimum(f, 0.0) - jnp.log1p(jnp.exp(-jnp.abs(f)))
        logf = jnp.where(lane < N_HEADS, logf, 0.0)
        pieces = jnp.concatenate(_split3(logf)[:2], axis=1)
        cum = _dot(tri_ref[...], pieces)
        c = cum[:, 0:LANES] + cum[:, LANES:2 * LANES] + carry
        carry = c[sub - 1:, :]

        ap = jnp.concatenate(_split3(jnp.where(lane == N_HEADS, 1.0, c * LOG2E)), axis=1)
        ex = _dot(ap, pqk_ref[...])
        exq, exk = ex[:, 0:LANES], ex[:, LANES:2 * LANES]

        q = _dot(h, wqkv_ref[:, 0:D_ATTN])
        k = _dot(h, wqkv_ref[:, D_ATTN:2 * D_ATTN])
        ssq = _dot(_bf16(q * q), eh_ref[...])
        ssk = _dot(_bf16(k * k), eh_ref[...])
        qn = q * lax.rsqrt(ssq * (1.0 / HEAD_DIM) + EPS) * gq_ref[...]
        kn = k * lax.rsqrt(ssk * (1.0 / HEAD_DIM) + EPS) * gk_ref[...]

        for hd in range(N_HEADS):
            sl = slice((hd // 2) * LANES, (hd // 2 + 1) * LANES)
            blk_q, blk_k = qn[:, sl], kn[:, sl]
            if hd % 2:
                blk_q = pltpu.roll(blk_q, HEAD_DIM, axis=1)
                blk_k = pltpu.roll(blk_k, HEAD_DIM, axis=1)
            shift = HEAD_DIM - N_PIECES * hd
            qh = jnp.where(lane < HEAD_DIM, blk_q, jnp.where(ex_keep, pltpu.roll(exq, shift, axis=1), 0.0))
            kh = jnp.where(lane < HEAD_DIM, blk_k, jnp.where(ex_keep, pltpu.roll(exk, shift, axis=1), 0.0))
            qt_ref[0, 0, hd * LANES:(hd + 1) * LANES, rows] = _bf16(qh.T)
            kh_ref[rows, hd * LANES:(hd + 1) * LANES] = _bf16(kh)

        v = _dot(h, wqkv_ref[:, 2 * D_ATTN:3 * D_ATTN])
        for pr in range(N_HEADS // 2):
            vt = _bf16(v[:, pr * LANES:(pr + 1) * LANES].T)
            for hh in range(2):
                base = (2 * pr + hh) * VT_ROWS
                vt_ref[0, 0, base:base + HEAD_DIM, rows] = vt[hh * HEAD_DIM:(hh + 1) * HEAD_DIM, :]
                vt_ref[0, 0, base + HEAD_DIM:base + VT_ROWS, rows] = ones_rows

        r = _dot(h, wc_ref[...])
        cp_ref[rows, 0:D_CONV] = _bf16(r[:, C_CC:C_CC + D_CONV] * r[:, C_CX:C_CX + D_CONV])
        cp_ref[rows, D_CONV:2 * D_CONV] = _bf16(r[:, C_CB:C_CB + D_CONV])
        cp_ref[rows, 2 * D_CONV:2 * D_CONV + D_POOL] = _bf16(r[:, C_PX:C_PX + D_POOL])

    carry_ref[...] = carry


def _const_spec(shape):
    return pl.BlockSpec(shape, lambda *_: (0,) * len(shape), pipeline_mode=pl.Buffered(1))


def _layer_spec(layer, shape):
    return pl.BlockSpec((None,) + shape, lambda *_: (layer,) + (0,) * len(shape), pipeline_mode=pl.Buffered(1))


def _proj_call(layer, x2, g, wqkv, wf, wc, fb, gq, gk, eh, tri, pqk, *, seq):
    t = x2.shape[0]
    tm = ROW_TILE
    row = lambda i: (i, 0)
    outs = [
        jax.ShapeDtypeStruct((t // seq, seq // tm, N_HEADS * LANES, tm), jnp.bfloat16),
        jax.ShapeDtypeStruct((t, N_HEADS * LANES), jnp.bfloat16),
        jax.ShapeDtypeStruct((t // seq, seq // tm, N_HEADS * VT_ROWS, tm), jnp.bfloat16),
        jax.ShapeDtypeStruct((t, 2 * D_CONV + D_POOL), jnp.bfloat16),
    ]
    tile_t = lambda i: (i // (seq // tm), i % (seq // tm), 0, 0)
    return pl.pallas_call(
        functools.partial(_proj_kernel, tiles_per_seq=seq // tm),
        grid=(t // tm,),
        in_specs=[
            pl.BlockSpec((tm, D_MODEL), row),
            _layer_spec(layer, (1, D_MODEL)),
            _layer_spec(layer, (D_MODEL, 3 * D_ATTN)),
            _layer_spec(layer, (D_MODEL, LANES)),
            _layer_spec(layer, (D_MODEL, 3 * D_CONV + D_POOL)),
            _layer_spec(layer, (1, LANES)),
            _layer_spec(layer, (1, D_ATTN)),
            _layer_spec(layer, (1, D_ATTN)),
            _const_spec((D_ATTN, D_ATTN)),
            _const_spec((SUB_ROWS, SUB_ROWS)),
            _const_spec((N_PIECES * LANES, 2 * LANES)),
        ],
        out_specs=[
            pl.BlockSpec((1, 1, N_HEADS * LANES, tm), tile_t),
            pl.BlockSpec((tm, N_HEADS * LANES), row),
            pl.BlockSpec((1, 1, N_HEADS * VT_ROWS, tm), tile_t),
            pl.BlockSpec((tm, 2 * D_CONV + D_POOL), row),
        ],
        out_shape=outs,
        scratch_shapes=[pltpu.VMEM((1, LANES), jnp.float32)],
        compiler_params=pltpu.CompilerParams(dimension_semantics=("arbitrary",), vmem_limit_bytes=VMEM_LIMIT),
        name="proj",
    )(x2, g, wqkv, wf, wc, fb, gq, gk, eh, tri, pqk)


def _attn_kernel(qt_ref, kh_ref, vt_ref, o_ref, acc_sc, aux_sc, *, online):
    i = pl.program_id(1)
    tq, tk = Q_TILE, K_TILE
    heads = range(N_HEADS)

    acc_sc[...] = jnp.zeros_like(acc_sc)
    m_sc = p_sc = aux_sc
    if online:
        m_sc[...] = jnp.full_like(m_sc, NEG_BIG)

    def logits(kt, hh, masked, k0=0, k1=tk, q0=0):
        start = pl.multiple_of(kt * tk + k0, math.gcd(tk, k0))
        s = _dot(kh_ref[0, pl.ds(start, k1 - k0), hh * LANES:(hh + 1) * LANES],
                 qt_ref[0, 0, hh * LANES:(hh + 1) * LANES, q0:])
        if masked:
            k_id = lax.broadcasted_iota(jnp.int32, s.shape, 0) + (kt - i) * tk + k0
            q_id = lax.broadcasted_iota(jnp.int32, s.shape, 1) + q0
            s = jnp.where(k_id <= q_id, s, NEG_BIG)
        return s

    def vt_tile(kt, hh):
        return vt_ref[0, kt, hh * VT_ROWS:(hh + 1) * VT_ROWS, :]

    if online:
        def step(kt, masked):
            for hh in heads:
                s = logits(kt, hh, masked)
                m_prev = m_sc[hh]
                m_new = jnp.maximum(m_prev, jnp.max(s, axis=0, keepdims=True))
                m_sc[hh] = m_new
                acc_sc[hh] = jnp.exp2(m_prev - m_new) * acc_sc[hh] + _dot(vt_tile(kt, hh), _bf16(jnp.exp2(s - m_new)))

        def body(kt, carry):
            step(kt, False)
            return carry

        lax.fori_loop(0, i, body, 0)
        step(i, True)
    else:
        def stage(kt, rd, wr):
            for hh in heads:
                p_sc[wr, hh] = _bf16(jnp.exp2(logits(kt + 1, hh, False)))
                acc_sc[hh] += _dot(vt_tile(kt, hh), p_sc[rd, hh])

        for hh in heads:
            p0 = _bf16(jnp.exp2(logits(0, hh, True)))
            p_sc[0, hh] = p0
            p_sc[1, hh] = p0

        n_plain = jnp.maximum(i - 1, 0)

        @pl.when(n_plain % 2 == 1)
        def _():
            stage(0, 0, 1)

        def body(j, carry):
            kt = n_plain % 2 + 2 * j
            stage(kt, 1, 0)
            stage(kt + 1, 0, 1)
            return carry

        lax.fori_loop(0, n_plain // 2, body, 0)

        half = tk // 2

        @pl.when(i > 0)
        def _():
            for hh in heads:
                p_sc[0, hh, 0:half, :] = _bf16(jnp.exp2(logits(i, hh, True, 0, half)))
                p_sc[0, hh, half:tk, half:tq] = _bf16(jnp.exp2(logits(i, hh, True, half, tk, half)))
                acc_sc[hh] += _dot(vt_tile(i - 1, hh), p_sc[1, hh])

        for hh in heads:
            vt = vt_tile(i, hh)
            early = _dot(vt[:, 0:half], p_sc[0, hh, 0:half, :])
            late = _dot(vt[:, half:tk], p_sc[0, hh, half:tk, half:tq])
            acc_sc[hh, :, 0:half] += early[:, 0:half]
            acc_sc[hh, :, half:tq] += early[:, half:tq] + late

    for pr in range(N_HEADS // 2):
        o_t = []
        for hh in (2 * pr, 2 * pr + 1):
            acc = acc_sc[hh]
            o_t.append(acc[0:HEAD_DIM, :] / acc[HEAD_DIM:HEAD_DIM + 1, :])
        o_ref[0, :, pr * LANES:(pr + 1) * LANES] = _bf16(jnp.concatenate(o_t, axis=0).T)


def _attn_call(qt, kh, vt, *, online):
    b, s, _ = kh.shape
    tq = Q_TILE
    scratch = [pltpu.VMEM((N_HEADS, VT_ROWS, tq), jnp.float32),
               pltpu.VMEM((N_HEADS, 1, tq), jnp.float32) if online else pltpu.VMEM((2, N_HEADS, K_TILE, tq), jnp.bfloat16)]
    return pl.pallas_call(
        functools.partial(_attn_kernel, online=online),
        grid=(b, s // tq),
        in_specs=[
            pl.BlockSpec((1, 1, N_HEADS * LANES, tq), lambda bi, i: (bi, i, 0, 0)),
            pl.BlockSpec((1, s, N_HEADS * LANES), lambda bi, i: (bi, 0, 0)),
            pl.BlockSpec((1, s // K_TILE, N_HEADS * VT_ROWS, K_TILE), lambda bi, i: (bi, 0, 0, 0)),
        ],
        out_specs=pl.BlockSpec((1, tq, D_ATTN), lambda bi, i: (bi, i, 0)),
        out_shape=jax.ShapeDtypeStruct((b, s, D_ATTN), jnp.bfloat16),
        scratch_shapes=scratch,
        compiler_params=pltpu.CompilerParams(
            dimension_semantics=("arbitrary", "arbitrary"), vmem_limit_bytes=VMEM_LIMIT),
        name="attn_online" if online else "attn",
    )(qt, kh, vt)


def _shift_rows(x, k):
    return pltpu.roll(x, k, axis=0)


def _branch_inputs(cp_ref, halo_ref, cw_ref, seq_tile):
    tm = cp_ref.shape[0]
    halo = halo_ref[...].astype(jnp.float32) * jnp.where(seq_tile == 0, 0.0, 1.0)
    cp = cp_ref[...].astype(jnp.float32)
    z = jnp.concatenate([halo[:, 0:D_CONV], cp[:, 0:D_CONV]], axis=0)
    px = jnp.concatenate([halo[:, 2 * D_CONV:], cp[:, 2 * D_CONV:]], axis=0)
    cb = cp[:, D_CONV:2 * D_CONV]

    cw = cw_ref[...]
    conv = cw[2:3, :] * z + cw[1:2, :] * _shift_rows(z, 1) + cw[0:1, :] * _shift_rows(z, 2)
    u = _bf16(cb * conv[HALO:, :])

    s2 = px + _shift_rows(px, 1)
    s4 = s2 + _shift_rows(s2, 2)
    s8 = s4 + _shift_rows(s4, 4)
    s16 = s8 + _shift_rows(s8, 8)
    grp = lax.broadcasted_iota(jnp.int32, (tm, D_POOL), 1) // POOL_GROUP_DIM
    wsum = jnp.where(grp == 0, s2[HALO:], jnp.where(grp == 1, s4[HALO:], jnp.where(grp == 2, s8[HALO:], s16[HALO:])))
    pos = seq_tile * tm + lax.broadcasted_iota(jnp.int32, (tm, D_POOL), 0)
    win = jnp.left_shift(2, grp)
    counts = jnp.minimum(pos + 1, win).astype(jnp.float32)
    d = _bf16(wsum / counts - px[HALO:])
    return u, d


def _merge_rows(x, a, u, d, g_ref, wg_ref, wao_ref, wco_ref, wpool_ref, ps_ref, wo_ref):
    h = _bf16(x * _rms_scale(x) * g_ref[...])
    y_attn = _dot(a, wao_ref[...])
    y_conv = _dot(u, wco_ref[...])
    y_pool = _dot(d, wpool_ref[...]) * ps_ref[...]
    merged = jax.nn.sigmoid(_dot(h, wg_ref[:, 0:D_MODEL])) * y_attn
    merged += jax.nn.sigmoid(_dot(h, wg_ref[:, D_MODEL:2 * D_MODEL])) * y_conv
    merged += jax.nn.sigmoid(_dot(h, wg_ref[:, 2 * D_MODEL:3 * D_MODEL])) * y_pool
    return x + _dot(_bf16(merged), wo_ref[...])


def _ffn_rows(x, g_ref, wi_ref, wo_ref):
    h = _bf16(x * _rms_scale(x) * g_ref[...])
    acc = x
    for c0, c1 in FF_CHUNKS:
        gt = _dot(h, wi_ref[:, c0:c1])
        up = _dot(h, wi_ref[:, D_FF + c0:D_FF + c1])
        act = _bf16(gt * jax.nn.sigmoid(gt) * up)
        acc = acc + _dot(act, wo_ref[c0:c1, :])
    return acc


def _mix_ffn_kernel(x_ref, a_ref, cp_ref, halo_ref, g_ref, wg_ref, wao_ref, cw_ref, wco_ref, wpool_ref, ps_ref, wo_ref,
                    g2_ref, wi_ref, wfo_ref, o_ref, *, tiles_per_seq):
    seq_tile = pl.program_id(0) % tiles_per_seq
    u, d = _branch_inputs(cp_ref, halo_ref, cw_ref, seq_tile)
    for r0 in range(0, x_ref.shape[0], SUB_ROWS):
        rows = slice(r0, r0 + SUB_ROWS)
        x1 = _merge_rows(x_ref[rows, :], a_ref[rows, :], u[rows], d[rows], g_ref, wg_ref, wao_ref, wco_ref, wpool_ref,
                         ps_ref, wo_ref)
        o_ref[rows, :] = _ffn_rows(x1, g2_ref, wi_ref, wfo_ref)


def _mix_ffn_call(layer, x2, a2, cp, g, wg, wao, cw, wco, wpool, ps, wo, g2, wi, wfo, *, seq):
    t = x2.shape[0]
    tm = ROW_TILE
    row = lambda i: (i, 0)
    cpw = 2 * D_CONV + D_POOL
    halo_blocks = tm // HALO
    return pl.pallas_call(
        functools.partial(_mix_ffn_kernel, tiles_per_seq=seq // tm),
        grid=(t // tm,),
        in_specs=[
            pl.BlockSpec((tm, D_MODEL), row),
            pl.BlockSpec((tm, D_ATTN), row),
            pl.BlockSpec((tm, cpw), row),
            pl.BlockSpec((HALO, cpw), lambda i: (jnp.maximum(i * halo_blocks - 1, 0), 0)),
            _layer_spec(layer, (1, D_MODEL)),
            _layer_spec(layer, (D_MODEL, 3 * D_MODEL)),
            _layer_spec(layer, (D_ATTN, D_MODEL)),
            _layer_spec(layer, (8, D_CONV)),
            _layer_spec(layer, (D_CONV, D_MODEL)),
            _layer_spec(layer, (D_POOL, D_MODEL)),
            _layer_spec(layer, (1, D_MODEL)),
            _layer_spec(layer, (D_MODEL, D_MODEL)),
            _layer_spec(layer, (1, D_MODEL)),
            _layer_spec(layer, (D_MODEL, 2 * D_FF)),
            _layer_spec(layer, (D_FF, D_MODEL)),
        ],
        out_specs=pl.BlockSpec((tm, D_MODEL), row),
        out_shape=jax.ShapeDtypeStruct((t, D_MODEL), jnp.float32),
        compiler_params=pltpu.CompilerParams(dimension_semantics=("arbitrary",), vmem_limit_bytes=VMEM_LIMIT),
        name="mix_ffn",
    )(x2, a2, cp, cp, g, wg, wao, cw, wco, wpool, ps, wo, g2, wi, wfo)


def _split_w_in_kernel(w_ref, qkv_ref, f_ref, c_ref, g_ref):
    qkv_ref[0] = _bf16(w_ref[0, :, 0:IN_F])
    f_ref[0] = jnp.zeros(f_ref.shape[1:], jnp.bfloat16)
    f_ref[0, :, 0:N_HEADS] = _bf16(w_ref[0, :, IN_F:IN_C])
    c_ref[0] = _bf16(w_ref[0, :, IN_C:IN_G])
    g_ref[0] = _bf16(w_ref[0, :, IN_G:])


def _split_w_in_call(w_in):
    depth, d_in, cols = w_in.shape
    rt = W_IN_ROWS
    blk = lambda n: pl.BlockSpec((1, rt, n), lambda l, r: (l, r, 0))
    widths = (IN_F, LANES, IN_G - IN_C, cols - IN_G)
    return pl.pallas_call(
        _split_w_in_kernel,
        grid=(depth, d_in // rt),
        in_specs=[blk(cols)],
        out_specs=[blk(n) for n in widths],
        out_shape=[jax.ShapeDtypeStruct((depth, d_in, n), jnp.bfloat16) for n in widths],
        compiler_params=pltpu.CompilerParams(dimension_semantics=("arbitrary", "arbitrary"),
                                             vmem_limit_bytes=VMEM_LIMIT),
        name="split_w_in",
    )(w_in)


def _placement_matrices():
    pq = np.zeros((N_PIECES * LANES, LANES), np.float32)
    pk = np.zeros((N_PIECES * LANES, LANES), np.float32)
    ones_row = N_HEADS
    for h in range(N_HEADS):
        for j in range(N_PIECES):
            pq[j * LANES + h, N_PIECES * h + j] = 1.0
            pk[ones_row, N_PIECES * h + j] = 1.0
            pq[ones_row, X_ONES + N_PIECES * h + j] = 1.0
            pk[j * LANES + h, X_ONES + N_PIECES * h + j] = -1.0
    return jnp.asarray(np.concatenate([pq, pk], axis=1), jnp.bfloat16)


def kernel(x, norm_mix_g, w_in, forget_b, q_norm_g, k_norm_g, w_attn_out, conv_w, w_conv_out, pool_w, pool_scale,
           w_o, norm_ffn_g, w_ffn_in, w_ffn_out):
    bsz, seq, _ = x.shape
    depth = w_in.shape[0]
    assert seq % ROW_TILE == 0 and seq % Q_TILE == 0 and Q_TILE == K_TILE and ROW_TILE == K_TILE

    bf = jnp.bfloat16
    eh = jnp.asarray(np.kron(np.eye(N_HEADS), np.ones((HEAD_DIM, HEAD_DIM))), bf)
    tri = jnp.asarray(np.tril(np.ones((SUB_ROWS, SUB_ROWS))), bf)
    pqk = _placement_matrices()

    row3 = lambda p: p[:, None, :]
    w_qkv, w_f, w_c, w_g = _split_w_in_call(w_in)
    fb = row3(jnp.pad(forget_b, ((0, 0), (0, LANES - N_HEADS))))
    q_scale = (HEAD_DIM ** -0.5) * LOG2E
    gq = row3(jnp.tile(q_norm_g, (1, N_HEADS))) * q_scale
    gk = row3(jnp.tile(k_norm_g, (1, N_HEADS)))
    g_mix, g_ffn, p_scale = row3(norm_mix_g), row3(norm_ffn_g), row3(pool_scale)
    wao = w_attn_out.astype(bf)
    cw = jnp.pad(conv_w, ((0, 0), (0, 8 - CONV_K), (0, 0)))
    wco = w_conv_out.astype(bf)
    grp_mask = jnp.asarray(np.kron(np.eye(len(POOL_WINDOWS)), np.ones((POOL_GROUP_DIM, POOL_OUT_DIM))), jnp.float32)
    wpool = (jnp.tile(pool_w.reshape(depth, D_POOL, POOL_OUT_DIM), (1, 1, len(POOL_WINDOWS))) * grp_mask).astype(bf)
    wo = w_o.astype(bf)
    wfi = w_ffn_in.astype(bf)
    wfo = w_ffn_out.astype(bf)

    x2 = x.reshape(bsz * seq, D_MODEL)
    for l in range(depth):
        qt, kh, vt, cp = _proj_call(l, x2, g_mix, w_qkv, w_f, w_c, fb, gq, gk, eh, tri, pqk, seq=seq)
        logit_bound = HEAD_DIM * jnp.max(jnp.abs(gq[l])) * jnp.max(jnp.abs(gk[l]))
        a = lax.cond(logit_bound < UNSHIFTED_LOGIT_LIMIT,
                     functools.partial(_attn_call, online=False), functools.partial(_attn_call, online=True),
                     qt, kh.reshape(bsz, seq, N_HEADS * LANES), vt)
        x2 = _mix_ffn_call(l, x2, a.reshape(bsz * seq, D_ATTN), cp, g_mix, w_g, wao, cw, wco, wpool, p_scale, wo,
                           g_ffn, wfi, wfo, seq=seq)
    return x2.reshape(bsz, seq, D_MODEL)
```

```python
import functools
import math

import numpy as np
import jax
import jax.numpy as jnp
from jax import lax
from jax.experimental import pallas as pl
from jax.experimental.pallas import tpu as pltpu

D_MODEL = 1024
HEAD_DIM = 64
D_ATTN = D_MODEL // 2
N_HEADS = D_ATTN // HEAD_DIM
D_CONV = D_MODEL // 4
CONV_K = 3
D_POOL = D_MODEL // 4
POOL_WINDOWS = (2, 4, 8, 16)
POOL_GROUP_DIM = D_POOL // len(POOL_WINDOWS)
POOL_OUT_DIM = D_MODEL // len(POOL_WINDOWS)
D_FF = -(-8 * D_MODEL // (3 * 256)) * 256
EPS = 1e-6

LANES = 128
VT_ROWS = HEAD_DIM + 16
HALO = 16
N_PIECES = 3
LOG2E = 1.4426950408889634
NEG_BIG = -1e30
UNSHIFTED_LOGIT_LIMIT = 100.0

ROW_TILE = 512
SUB_ROWS = 256
Q_TILE = 512
K_TILE = 512
FF_CHUNKS = ((0, 1024), (1024, 2048), (2048, D_FF))
VMEM_LIMIT = 56 * 1024 * 1024

IN_F = 3 * D_ATTN
IN_C = IN_F + N_HEADS
IN_G = IN_C + 3 * D_CONV + D_POOL
C_CX, C_CB, C_CC, C_PX = 0, D_CONV, 2 * D_CONV, 3 * D_CONV

X_ONES = N_HEADS * N_PIECES


def _bf16(x):
    return x.astype(jnp.bfloat16)


def _dot(a, b):
    return jnp.dot(a, b, preferred_element_type=jnp.float32)


def _split3(x):
    hi = _bf16(x)
    r1 = x - hi.astype(jnp.float32)
    mid = _bf16(r1)
    lo = _bf16(r1 - mid.astype(jnp.float32))
    return hi, mid, lo


def _rms_scale(x):
    return lax.rsqrt(jnp.mean(x * x, axis=-1, keepdims=True) + EPS)


def _proj_kernel(x_ref, g_ref, wqkv_ref, wf_ref, wc_ref, fb_ref, gq_ref, gk_ref, eh_ref, tri_ref, pqk_ref,
                 qt_ref, kh_ref, vt_ref, cp_ref, carry_ref, *, tiles_per_seq):
    i = pl.program_id(0)

    @pl.when(i % tiles_per_seq == 0)
    def _():
        carry_ref[...] = jnp.zeros_like(carry_ref)

    x = x_ref[...]
    h = _bf16(x * _rms_scale(x) * g_ref[...])
    tm = x.shape[0]
    lane = lax.broadcasted_iota(jnp.int32, (tm, LANES), 1)

    f = jnp.concatenate([_dot(h[0:tm // 2], wf_ref[...]), _dot(h[tm // 2:tm], wf_ref[...])], axis=0) + fb_ref[...]
    logf = jnp.minimum(f, 0.0) - jnp.log1p(jnp.exp(-jnp.abs(f)))
    logf = jnp.where(lane < N_HEADS, logf, 0.0)
    pieces = jnp.concatenate(_split3(logf)[:2], axis=1)
    cum = _dot(tri_ref[...], pieces)
    c = cum[:, 0:LANES] + cum[:, LANES:2 * LANES] + carry_ref[...]
    carry_ref[...] = c[tm - 1:, :]

    ap = jnp.concatenate(_split3(jnp.where(lane == N_HEADS, 1.0, c * LOG2E)), axis=1)
    ex = _dot(ap, pqk_ref[...])
    exq, exk = ex[:, 0:LANES], ex[:, LANES:2 * LANES]

    q = _dot(h, wqkv_ref[:, 0:D_ATTN])
    k = _dot(h, wqkv_ref[:, D_ATTN:2 * D_ATTN])
    ssq = _dot(_bf16(q * q), eh_ref[...])
    ssk = _dot(_bf16(k * k), eh_ref[...])
    qn = q * lax.rsqrt(ssq * (1.0 / HEAD_DIM) + EPS) * gq_ref[...]
    kn = k * lax.rsqrt(ssk * (1.0 / HEAD_DIM) + EPS) * gk_ref[...]

    ex_keep = ((lane >= HEAD_DIM) & (lane < HEAD_DIM + N_PIECES)) \
        | ((lane >= HEAD_DIM + X_ONES) & (lane < HEAD_DIM + X_ONES + N_PIECES))
    for hd in range(N_HEADS):
        sl = slice((hd // 2) * LANES, (hd // 2 + 1) * LANES)
        blk_q, blk_k = qn[:, sl], kn[:, sl]
        if hd % 2:
            blk_q = pltpu.roll(blk_q, HEAD_DIM, axis=1)
            blk_k = pltpu.roll(blk_k, HEAD_DIM, axis=1)
        shift = HEAD_DIM - N_PIECES * hd
        qh = jnp.where(lane < HEAD_DIM, blk_q, jnp.where(ex_keep, pltpu.roll(exq, shift, axis=1), 0.0))
        kh = jnp.where(lane < HEAD_DIM, blk_k, jnp.where(ex_keep, pltpu.roll(exk, shift, axis=1), 0.0))
        qt_ref[0, 0, hd * LANES:(hd + 1) * LANES, :] = _bf16(qh.T)
        kh_ref[:, hd * LANES:(hd + 1) * LANES] = _bf16(kh)

    v = _dot(h, wqkv_ref[:, 2 * D_ATTN:3 * D_ATTN])
    ones_rows = _bf16(jnp.where(lax.broadcasted_iota(jnp.int32, (VT_ROWS - HEAD_DIM, tm), 0) == 0, 1.0, 0.0))
    for pr in range(N_HEADS // 2):
        vt = _bf16(v[:, pr * LANES:(pr + 1) * LANES].T)
        for hh in range(2):
            base = (2 * pr + hh) * VT_ROWS
            vt_ref[0, 0, base:base + HEAD_DIM, :] = vt[hh * HEAD_DIM:(hh + 1) * HEAD_DIM, :]
            vt_ref[0, 0, base + HEAD_DIM:base + VT_ROWS, :] = ones_rows

    r = _dot(h, wc_ref[...])
    cp_ref[:, 0:D_CONV] = _bf16(r[:, C_CC:C_CC + D_CONV] * r[:, C_CX:C_CX + D_CONV])
    cp_ref[:, D_CONV:2 * D_CONV] = _bf16(r[:, C_CB:C_CB + D_CONV])
    cp_ref[:, 2 * D_CONV:2 * D_CONV + D_POOL] = _bf16(r[:, C_PX:C_PX + D_POOL])


def _const_spec(shape):
    return pl.BlockSpec(shape, lambda *_: (0,) * len(shape), pipeline_mode=pl.Buffered(1))


def _layer_spec(layer, shape):
    return pl.BlockSpec((None,) + shape, lambda *_: (layer,) + (0,) * len(shape), pipeline_mode=pl.Buffered(1))


def _proj_call(layer, x2, g, wqkv, wf, wc, fb, gq, gk, eh, tri, pqk, *, seq):
    t = x2.shape[0]
    tm = ROW_TILE
    row = lambda i: (i, 0)
    outs = [
        jax.ShapeDtypeStruct((t // seq, seq // tm, N_HEADS * LANES, tm), jnp.bfloat16),
        jax.ShapeDtypeStruct((t, N_HEADS * LANES), jnp.bfloat16),
        jax.ShapeDtypeStruct((t // seq, seq // tm, N_HEADS * VT_ROWS, tm), jnp.bfloat16),
        jax.ShapeDtypeStruct((t, 2 * D_CONV + D_POOL), jnp.bfloat16),
    ]
    tile_t = lambda i: (i // (seq // tm), i % (seq // tm), 0, 0)
    return pl.pallas_call(
        functools.partial(_proj_kernel, tiles_per_seq=seq // tm),
        grid=(t // tm,),
        in_specs=[
            pl.BlockSpec((tm, D_MODEL), row),
            _layer_spec(layer, (1, D_MODEL)),
            _layer_spec(layer, (D_MODEL, 3 * D_ATTN)),
            _layer_spec(layer, (D_MODEL, LANES)),
            _layer_spec(layer, (D_MODEL, 3 * D_CONV + D_POOL)),
            _layer_spec(layer, (1, LANES)),
            _layer_spec(layer, (1, D_ATTN)),
            _layer_spec(layer, (1, D_ATTN)),
            _const_spec((D_ATTN, D_ATTN)),
            _const_spec((tm, tm)),
            _const_spec((N_PIECES * LANES, 2 * LANES)),
        ],
        out_specs=[
            pl.BlockSpec((1, 1, N_HEADS * LANES, tm), tile_t),
            pl.BlockSpec((tm, N_HEADS * LANES), row),
            pl.BlockSpec((1, 1, N_HEADS * VT_ROWS, tm), tile_t),
            pl.BlockSpec((tm, 2 * D_CONV + D_POOL), row),
        ],
        out_shape=outs,
        scratch_shapes=[pltpu.VMEM((1, LANES), jnp.float32)],
        compiler_params=pltpu.CompilerParams(dimension_semantics=("arbitrary",), vmem_limit_bytes=VMEM_LIMIT),
        name="proj",
    )(x2, g, wqkv, wf, wc, fb, gq, gk, eh, tri, pqk)


def _attn_kernel(qt_ref, kh_ref, vt_ref, o_ref, acc_sc, aux_sc, *, online):
    i = pl.program_id(1)
    tq, tk = Q_TILE, K_TILE
    heads = range(N_HEADS)

    acc_sc[...] = jnp.zeros_like(acc_sc)
    m_sc = p_sc = aux_sc
    if online:
        m_sc[...] = jnp.full_like(m_sc, NEG_BIG)

    def logits(kt, hh, masked, k0=0, k1=tk, q0=0):
        start = pl.multiple_of(kt * tk + k0, math.gcd(tk, k0))
        s = _dot(kh_ref[0, pl.ds(start, k1 - k0), hh * LANES:(hh + 1) * LANES],
                 qt_ref[0, 0, hh * LANES:(hh + 1) * LANES, q0:])
        if masked:
            k_id = lax.broadcasted_iota(jnp.int32, s.shape, 0) + (kt - i) * tk + k0
            q_id = lax.broadcasted_iota(jnp.int32, s.shape, 1) + q0
            s = jnp.where(k_id <= q_id, s, NEG_BIG)
        return s

    def vt_tile(kt, hh):
        return vt_ref[0, kt, hh * VT_ROWS:(hh + 1) * VT_ROWS, :]

    if online:
        def step(kt, masked):
            for hh in heads:
                s = logits(kt, hh, masked)
                m_prev = m_sc[hh]
                m_new = jnp.maximum(m_prev, jnp.max(s, axis=0, keepdims=True))
                m_sc[hh] = m_new
                acc_sc[hh] = jnp.exp2(m_prev - m_new) * acc_sc[hh] + _dot(vt_tile(kt, hh), _bf16(jnp.exp2(s - m_new)))

        def body(kt, carry):
            step(kt, False)
            return carry

        lax.fori_loop(0, i, body, 0)
        step(i, True)
    else:
        def stage(kt, rd, wr):
            for hh in heads:
                p_sc[wr, hh] = _bf16(jnp.exp2(logits(kt + 1, hh, False)))
                acc_sc[hh] += _dot(vt_tile(kt, hh), p_sc[rd, hh])

        for hh in heads:
            p0 = _bf16(jnp.exp2(logits(0, hh, True)))
            p_sc[0, hh] = p0
            p_sc[1, hh] = p0

        n_plain = jnp.maximum(i - 1, 0)

        @pl.when(n_plain % 2 == 1)
        def _():
            stage(0, 0, 1)

        def body(j, carry):
            kt = n_plain % 2 + 2 * j
            stage(kt, 1, 0)
            stage(kt + 1, 0, 1)
            return carry

        lax.fori_loop(0, n_plain // 2, body, 0)

        half = tk // 2

        @pl.when(i > 0)
        def _():
            for hh in heads:
                p_sc[0, hh, 0:half, :] = _bf16(jnp.exp2(logits(i, hh, True, 0, half)))
                p_sc[0, hh, half:tk, half:tq] = _bf16(jnp.exp2(logits(i, hh, True, half, tk, half)))
                acc_sc[hh] += _dot(vt_tile(i - 1, hh), p_sc[1, hh])

        for hh in heads:
            vt = vt_tile(i, hh)
            early = _dot(vt[:, 0:half], p_sc[0, hh, 0:half, :])
            late = _dot(vt[:, half:tk], p_sc[0, hh, half:tk, half:tq])
            acc_sc[hh, :, 0:half] += early[:, 0:half]
            acc_sc[hh, :, half:tq] += early[:, half:tq] + late

    for pr in range(N_HEADS // 2):
        o_t = []
        for hh in (2 * pr, 2 * pr + 1):
            acc = acc_sc[hh]
            o_t.append(acc[0:HEAD_DIM, :] / acc[HEAD_DIM:HEAD_DIM + 1, :])
        o_ref[0, :, pr * LANES:(pr + 1) * LANES] = _bf16(jnp.concatenate(o_t, axis=0).T)


def _attn_call(qt, kh, vt, *, online):
    b, s, _ = kh.shape
    tq = Q_TILE
    scratch = [pltpu.VMEM((N_HEADS, VT_ROWS, tq), jnp.float32),
               pltpu.VMEM((N_HEADS, 1, tq), jnp.float32) if online else pltpu.VMEM((2, N_HEADS, K_TILE, tq), jnp.bfloat16)]
    return pl.pallas_call(
        functools.partial(_attn_kernel, online=online),
        grid=(b, s // tq),
        in_specs=[
            pl.BlockSpec((1, 1, N_HEADS * LANES, tq), lambda bi, i: (bi, i, 0, 0)),
            pl.BlockSpec((1, s, N_HEADS * LANES), lambda bi, i: (bi, 0, 0)),
            pl.BlockSpec((1, s // K_TILE, N_HEADS * VT_ROWS, K_TILE), lambda bi, i: (bi, 0, 0, 0)),
        ],
        out_specs=pl.BlockSpec((1, tq, D_ATTN), lambda bi, i: (bi, i, 0)),
        out_shape=jax.ShapeDtypeStruct((b, s, D_ATTN), jnp.bfloat16),
        scratch_shapes=scratch,
        compiler_params=pltpu.CompilerParams(
            dimension_semantics=("arbitrary", "arbitrary"), vmem_limit_bytes=VMEM_LIMIT),
        name="attn_online" if online else "attn",
    )(qt, kh, vt)


def _shift_rows(x, k):
    return pltpu.roll(x, k, axis=0)


def _branch_inputs(cp_ref, halo_ref, cw_ref, seq_tile):
    tm = cp_ref.shape[0]
    halo = halo_ref[...].astype(jnp.float32) * jnp.where(seq_tile == 0, 0.0, 1.0)
    cp = cp_ref[...].astype(jnp.float32)
    z = jnp.concatenate([halo[:, 0:D_CONV], cp[:, 0:D_CONV]], axis=0)
    px = jnp.concatenate([halo[:, 2 * D_CONV:], cp[:, 2 * D_CONV:]], axis=0)
    cb = cp[:, D_CONV:2 * D_CONV]

    cw = cw_ref[...]
    conv = cw[2:3, :] * z + cw[1:2, :] * _shift_rows(z, 1) + cw[0:1, :] * _shift_rows(z, 2)
    u = _bf16(cb * conv[HALO:, :])

    s2 = px + _shift_rows(px, 1)
    s4 = s2 + _shift_rows(s2, 2)
    s8 = s4 + _shift_rows(s4, 4)
    s16 = s8 + _shift_rows(s8, 8)
    grp = lax.broadcasted_iota(jnp.int32, (tm, D_POOL), 1) // POOL_GROUP_DIM
    wsum = jnp.where(grp == 0, s2[HALO:], jnp.where(grp == 1, s4[HALO:], jnp.where(grp == 2, s8[HALO:], s16[HALO:])))
    pos = seq_tile * tm + lax.broadcasted_iota(jnp.int32, (tm, D_POOL), 0)
    win = jnp.left_shift(2, grp)
    counts = jnp.minimum(pos + 1, win).astype(jnp.float32)
    d = _bf16(wsum / counts - px[HALO:])
    return u, d


def _merge_rows(x, a, u, d, g_ref, wg_ref, wao_ref, wco_ref, wpool_ref, ps_ref, wo_ref):
    h = _bf16(x * _rms_scale(x) * g_ref[...])
    y_attn = _dot(a, wao_ref[...])
    y_conv = _dot(u, wco_ref[...])
    y_pool = _dot(d, wpool_ref[...]) * ps_ref[...]
    merged = jax.nn.sigmoid(_dot(h, wg_ref[:, 0:D_MODEL])) * y_attn
    merged += jax.nn.sigmoid(_dot(h, wg_ref[:, D_MODEL:2 * D_MODEL])) * y_conv
    merged += jax.nn.sigmoid(_dot(h, wg_ref[:, 2 * D_MODEL:3 * D_MODEL])) * y_pool
    return x + _dot(_bf16(merged), wo_ref[...])


def _ffn_rows(x, g_ref, wi_ref, wo_ref):
    h = _bf16(x * _rms_scale(x) * g_ref[...])
    acc = x
    for c0, c1 in FF_CHUNKS:
        gt = _dot(h, wi_ref[:, c0:c1])
        up = _dot(h, wi_ref[:, D_FF + c0:D_FF + c1])
        act = _bf16(gt * jax.nn.sigmoid(gt) * up)
        acc = acc + _dot(act, wo_ref[c0:c1, :])
    return acc


def _mix_ffn_kernel(x_ref, a_ref, cp_ref, halo_ref, g_ref, wg_ref, wao_ref, cw_ref, wco_ref, wpool_ref, ps_ref, wo_ref,
                    g2_ref, wi_ref, wfo_ref, o_ref, *, tiles_per_seq):
    seq_tile = pl.program_id(0) % tiles_per_seq
    u, d = _branch_inputs(cp_ref, halo_ref, cw_ref, seq_tile)
    for r0 in range(0, x_ref.shape[0], SUB_ROWS):
        rows = slice(r0, r0 + SUB_ROWS)
        x1 = _merge_rows(x_ref[rows, :], a_ref[rows, :], u[rows], d[rows], g_ref, wg_ref, wao_ref, wco_ref, wpool_ref,
                         ps_ref, wo_ref)
        o_ref[rows, :] = _ffn_rows(x1, g2_ref, wi_ref, wfo_ref)


def _mix_ffn_call(layer, x2, a2, cp, g, wg, wao, cw, wco, wpool, ps, wo, g2, wi, wfo, *, seq):
    t = x2.shape[0]
    tm = ROW_TILE
    row = lambda i: (i, 0)
    cpw = 2 * D_CONV + D_POOL
    halo_blocks = tm // HALO
    return pl.pallas_call(
        functools.partial(_mix_ffn_kernel, tiles_per_seq=seq // tm),
        grid=(t // tm,),
        in_specs=[
            pl.BlockSpec((tm, D_MODEL), row),
            pl.BlockSpec((tm, D_ATTN), row),
            pl.BlockSpec((tm, cpw), row),
            pl.BlockSpec((HALO, cpw), lambda i: (jnp.maximum(i * halo_blocks - 1, 0), 0)),
            _layer_spec(layer, (1, D_MODEL)),
            _layer_spec(layer, (D_MODEL, 3 * D_MODEL)),
            _layer_spec(layer, (D_ATTN, D_MODEL)),
            _layer_spec(layer, (8, D_CONV)),
            _layer_spec(layer, (D_CONV, D_MODEL)),
            _layer_spec(layer, (D_POOL, D_MODEL)),
            _layer_spec(layer, (1, D_MODEL)),
            _layer_spec(layer, (D_MODEL, D_MODEL)),
            _layer_spec(layer, (1, D_MODEL)),
            _layer_spec(layer, (D_MODEL, 2 * D_FF)),
            _layer_spec(layer, (D_FF, D_MODEL)),
        ],
        out_specs=pl.BlockSpec((tm, D_MODEL), row),
        out_shape=jax.ShapeDtypeStruct((t, D_MODEL), jnp.float32),
        compiler_params=pltpu.CompilerParams(dimension_semantics=("arbitrary",), vmem_limit_bytes=VMEM_LIMIT),
        name="mix_ffn",
    )(x2, a2, cp, cp, g, wg, wao, cw, wco, wpool, ps, wo, g2, wi, wfo)


def _placement_matrices():
    pq = np.zeros((N_PIECES * LANES, LANES), np.float32)
    pk = np.zeros((N_PIECES * LANES, LANES), np.float32)
    ones_row = N_HEADS
    for h in range(N_HEADS):
        for j in range(N_PIECES):
            pq[j * LANES + h, N_PIECES * h + j] = 1.0
            pk[ones_row, N_PIECES * h + j] = 1.0
            pq[ones_row, X_ONES + N_PIECES * h + j] = 1.0
            pk[j * LANES + h, X_ONES + N_PIECES * h + j] = -1.0
    return jnp.asarray(np.concatenate([pq, pk], axis=1), jnp.bfloat16)


def kernel(x, norm_mix_g, w_in, forget_b, q_norm_g, k_norm_g, w_attn_out, conv_w, w_conv_out, pool_w, pool_scale,
           w_o, norm_ffn_g, w_ffn_in, w_ffn_out):
    bsz, seq, _ = x.shape
    depth = w_in.shape[0]
    assert seq % ROW_TILE == 0 and seq % Q_TILE == 0 and Q_TILE == K_TILE and ROW_TILE == K_TILE

    bf = jnp.bfloat16
    eh = jnp.asarray(np.kron(np.eye(N_HEADS), np.ones((HEAD_DIM, HEAD_DIM))), bf)
    tri = jnp.asarray(np.tril(np.ones((ROW_TILE, ROW_TILE))), bf)
    pqk = _placement_matrices()

    row3 = lambda p: p[:, None, :]
    w_in_bf = lax.optimization_barrier(w_in.astype(bf))
    w_qkv = w_in_bf[:, :, :IN_F]
    w_f = jnp.pad(w_in_bf[:, :, IN_F:IN_C], ((0, 0), (0, 0), (0, LANES - N_HEADS)))
    w_c = w_in_bf[:, :, IN_C:IN_G]
    w_g = w_in_bf[:, :, IN_G:]
    fb = row3(jnp.pad(forget_b, ((0, 0), (0, LANES - N_HEADS))))
    q_scale = (HEAD_DIM ** -0.5) * LOG2E
    gq = row3(jnp.tile(q_norm_g, (1, N_HEADS))) * q_scale
    gk = row3(jnp.tile(k_norm_g, (1, N_HEADS)))
    g_mix, g_ffn, p_scale = row3(norm_mix_g), row3(norm_ffn_g), row3(pool_scale)
    wao = w_attn_out.astype(bf)
    cw = jnp.pad(conv_w, ((0, 0), (0, 8 - CONV_K), (0, 0)))
    wco = w_conv_out.astype(bf)
    grp_mask = jnp.asarray(np.kron(np.eye(len(POOL_WINDOWS)), np.ones((POOL_GROUP_DIM, POOL_OUT_DIM))), jnp.float32)
    wpool = (jnp.tile(pool_w.reshape(depth, D_POOL, POOL_OUT_DIM), (1, 1, len(POOL_WINDOWS))) * grp_mask).astype(bf)
    wo = w_o.astype(bf)
    wfi = w_ffn_in.astype(bf)
    wfo = w_ffn_out.astype(bf)

    x2 = x.reshape(bsz * seq, D_MODEL)
    for l in range(depth):
        qt, kh, vt, cp = _proj_call(l, x2, g_mix, w_qkv, w_f, w_c, fb, gq, gk, eh, tri, pqk, seq=seq)
        logit_bound = HEAD_DIM * jnp.max(jnp.abs(gq[l])) * jnp.max(jnp.abs(gk[l]))
        a = lax.cond(logit_bound < UNSHIFTED_LOGIT_LIMIT,
                     functools.partial(_attn_call, online=False), functools.partial(_attn_call, online=True),
                     qt, kh.reshape(bsz, seq, N_HEADS * LANES), vt)
        x2 = _mix_ffn_call(l, x2, a.reshape(bsz * seq, D_ATTN), cp, g_mix, w_g, wao, cw, wco, wpool, p_scale, wo,
                           g_ffn, wfi, wfo, seq=seq)
    return x2.reshape(bsz, seq, D_MODEL)
```

```python
import functools
import math

import numpy as np
import jax
import jax.numpy as jnp
from jax import lax
from jax.experimental import pallas as pl
from jax.experimental.pallas import tpu as pltpu

D_MODEL = 1024
HEAD_DIM = 64
D_ATTN = D_MODEL // 2
N_HEADS = D_ATTN // HEAD_DIM
D_CONV = D_MODEL // 4
CONV_K = 3
D_POOL = D_MODEL // 4
POOL_WINDOWS = (2, 4, 8, 16)
POOL_GROUP_DIM = D_POOL // len(POOL_WINDOWS)
POOL_OUT_DIM = D_MODEL // len(POOL_WINDOWS)
D_FF = -(-8 * D_MODEL // (3 * 256)) * 256
EPS = 1e-6

LANES = 128
VT_ROWS = HEAD_DIM + 16
HALO = 16
N_PIECES = 3
LOG2E = 1.4426950408889634
NEG_BIG = -1e30
UNSHIFTED_LOGIT_LIMIT = 100.0

ROW_TILE = 512
PROJ_TILE = 1024
SUB_ROWS = 256
Q_TILE = 512
K_TILE = 512
FF_CHUNKS = ((0, 1024), (1024, 2048), (2048, D_FF))
VMEM_LIMIT = 56 * 1024 * 1024

IN_F = 3 * D_ATTN
IN_C = IN_F + N_HEADS
IN_G = IN_C + 3 * D_CONV + D_POOL
C_CX, C_CB, C_CC, C_PX = 0, D_CONV, 2 * D_CONV, 3 * D_CONV

X_ONES = N_HEADS * N_PIECES


def _bf16(x):
    return x.astype(jnp.bfloat16)


def _dot(a, b):
    return jnp.dot(a, b, preferred_element_type=jnp.float32)


def _split3(x):
    hi = _bf16(x)
    r1 = x - hi.astype(jnp.float32)
    mid = _bf16(r1)
    lo = _bf16(r1 - mid.astype(jnp.float32))
    return hi, mid, lo


def _rms_scale(x):
    return lax.rsqrt(jnp.mean(x * x, axis=-1, keepdims=True) + EPS)


def _proj_kernel(x_ref, g_ref, wqkv_ref, wf_ref, wc_ref, fb_ref, gq_ref, gk_ref, eh_ref, tri_ref, pqk_ref,
                 qt_ref, kh_ref, vt_ref, cp_ref, carry_ref, *, tiles_per_seq):
    i = pl.program_id(0)

    @pl.when(i % tiles_per_seq == 0)
    def _():
        carry_ref[...] = jnp.zeros_like(carry_ref)

    sub = K_TILE
    lane = lax.broadcasted_iota(jnp.int32, (sub, LANES), 1)
    ex_keep = ((lane >= HEAD_DIM) & (lane < HEAD_DIM + N_PIECES)) \
        | ((lane >= HEAD_DIM + X_ONES) & (lane < HEAD_DIM + X_ONES + N_PIECES))
    ones_rows = _bf16(jnp.where(lax.broadcasted_iota(jnp.int32, (VT_ROWS - HEAD_DIM, sub), 0) == 0, 1.0, 0.0))
    carry = carry_ref[...]

    for j in range(x_ref.shape[0] // sub):
        rows = slice(j * sub, (j + 1) * sub)
        x = x_ref[rows, :]
        h = _bf16(x * _rms_scale(x) * g_ref[...])

        f = _dot(h, wf_ref[...]) + fb_ref[...]
        logf = jnp.minimum(f, 0.0) - jnp.log1p(jnp.exp(-jnp.abs(f)))
        logf = jnp.where(lane < N_HEADS, logf, 0.0)
        pieces = jnp.concatenate(_split3(logf)[:2], axis=1)
        cum = _dot(tri_ref[...], pieces)
        c = cum[:, 0:LANES] + cum[:, LANES:2 * LANES] + carry
        carry = c[sub - 1:, :]

        ap = jnp.concatenate(_split3(jnp.where(lane == N_HEADS, 1.0, c * LOG2E)), axis=1)
        ex = _dot(ap, pqk_ref[...])
        exq, exk = ex[:, 0:LANES], ex[:, LANES:2 * LANES]

        q = _dot(h, wqkv_ref[:, 0:D_ATTN])
        k = _dot(h, wqkv_ref[:, D_ATTN:2 * D_ATTN])
        ssq = _dot(_bf16(q * q), eh_ref[...])
        ssk = _dot(_bf16(k * k), eh_ref[...])
        qn = q * lax.rsqrt(ssq * (1.0 / HEAD_DIM) + EPS) * gq_ref[...]
        kn = k * lax.rsqrt(ssk * (1.0 / HEAD_DIM) + EPS) * gk_ref[...]

        for hd in range(N_HEADS):
            sl = slice((hd // 2) * LANES, (hd // 2 + 1) * LANES)
            blk_q, blk_k = qn[:, sl], kn[:, sl]
            if hd % 2:
                blk_q = pltpu.roll(blk_q, HEAD_DIM, axis=1)
                blk_k = pltpu.roll(blk_k, HEAD_DIM, axis=1)
            shift = HEAD_DIM - N_PIECES * hd
            qh = jnp.where(lane < HEAD_DIM, blk_q, jnp.where(ex_keep, pltpu.roll(exq, shift, axis=1), 0.0))
            kh = jnp.where(lane < HEAD_DIM, blk_k, jnp.where(ex_keep, pltpu.roll(exk, shift, axis=1), 0.0))
            qt_ref[0, j, hd * LANES:(hd + 1) * LANES, :] = _bf16(qh.T)
            kh_ref[rows, hd * LANES:(hd + 1) * LANES] = _bf16(kh)

        v = _dot(h, wqkv_ref[:, 2 * D_ATTN:3 * D_ATTN])
        for pr in range(N_HEADS // 2):
            vt = _bf16(v[:, pr * LANES:(pr + 1) * LANES].T)
            for hh in range(2):
                base = (2 * pr + hh) * VT_ROWS
                vt_ref[0, j, base:base + HEAD_DIM, :] = vt[hh * HEAD_DIM:(hh + 1) * HEAD_DIM, :]
                vt_ref[0, j, base + HEAD_DIM:base + VT_ROWS, :] = ones_rows

        r = _dot(h, wc_ref[...])
        cp_ref[rows, 0:D_CONV] = _bf16(r[:, C_CC:C_CC + D_CONV] * r[:, C_CX:C_CX + D_CONV])
        cp_ref[rows, D_CONV:2 * D_CONV] = _bf16(r[:, C_CB:C_CB + D_CONV])
        cp_ref[rows, 2 * D_CONV:2 * D_CONV + D_POOL] = _bf16(r[:, C_PX:C_PX + D_POOL])

    carry_ref[...] = carry


def _const_spec(shape):
    return pl.BlockSpec(shape, lambda *_: (0,) * len(shape), pipeline_mode=pl.Buffered(1))


def _layer_spec(layer, shape):
    return pl.BlockSpec((None,) + shape, lambda *_: (layer,) + (0,) * len(shape), pipeline_mode=pl.Buffered(1))


def _proj_call(layer, x2, g, wqkv, wf, wc, fb, gq, gk, eh, tri, pqk, *, seq):
    t = x2.shape[0]
    tm, kt = PROJ_TILE, K_TILE
    row = lambda i: (i, 0)
    outs = [
        jax.ShapeDtypeStruct((t // seq, seq // kt, N_HEADS * LANES, kt), jnp.bfloat16),
        jax.ShapeDtypeStruct((t, N_HEADS * LANES), jnp.bfloat16),
        jax.ShapeDtypeStruct((t // seq, seq // kt, N_HEADS * VT_ROWS, kt), jnp.bfloat16),
        jax.ShapeDtypeStruct((t, 2 * D_CONV + D_POOL), jnp.bfloat16),
    ]
    tile_t = lambda i: (i // (seq // tm), i % (seq // tm), 0, 0)
    return pl.pallas_call(
        functools.partial(_proj_kernel, tiles_per_seq=seq // tm),
        grid=(t // tm,),
        in_specs=[
            pl.BlockSpec((tm, D_MODEL), row),
            _layer_spec(layer, (1, D_MODEL)),
            _layer_spec(layer, (D_MODEL, 3 * D_ATTN)),
            _layer_spec(layer, (D_MODEL, LANES)),
            _layer_spec(layer, (D_MODEL, 3 * D_CONV + D_POOL)),
            _layer_spec(layer, (1, LANES)),
            _layer_spec(layer, (1, D_ATTN)),
            _layer_spec(layer, (1, D_ATTN)),
            _const_spec((D_ATTN, D_ATTN)),
            _const_spec((kt, kt)),
            _const_spec((N_PIECES * LANES, 2 * LANES)),
        ],
        out_specs=[
            pl.BlockSpec((1, tm // kt, N_HEADS * LANES, kt), tile_t),
            pl.BlockSpec((tm, N_HEADS * LANES), row),
            pl.BlockSpec((1, tm // kt, N_HEADS * VT_ROWS, kt), tile_t),
            pl.BlockSpec((tm, 2 * D_CONV + D_POOL), row),
        ],
        out_shape=outs,
        scratch_shapes=[pltpu.VMEM((1, LANES), jnp.float32)],
        compiler_params=pltpu.CompilerParams(dimension_semantics=("arbitrary",), vmem_limit_bytes=VMEM_LIMIT),
        name="proj",
    )(x2, g, wqkv, wf, wc, fb, gq, gk, eh, tri, pqk)


def _attn_kernel(qt_ref, kh_ref, vt_ref, o_ref, acc_sc, aux_sc, *, online):
    i = pl.program_id(1)
    tq, tk = Q_TILE, K_TILE
    heads = range(N_HEADS)

    acc_sc[...] = jnp.zeros_like(acc_sc)
    m_sc = p_sc = aux_sc
    if online:
        m_sc[...] = jnp.full_like(m_sc, NEG_BIG)

    def logits(kt, hh, masked, k0=0, k1=tk, q0=0):
        start = pl.multiple_of(kt * tk + k0, math.gcd(tk, k0))
        s = _dot(kh_ref[0, pl.ds(start, k1 - k0), hh * LANES:(hh + 1) * LANES],
                 qt_ref[0, 0, hh * LANES:(hh + 1) * LANES, q0:])
        if masked:
            k_id = lax.broadcasted_iota(jnp.int32, s.shape, 0) + (kt - i) * tk + k0
            q_id = lax.broadcasted_iota(jnp.int32, s.shape, 1) + q0
            s = jnp.where(k_id <= q_id, s, NEG_BIG)
        return s

    def vt_tile(kt, hh):
        return vt_ref[0, kt, hh * VT_ROWS:(hh + 1) * VT_ROWS, :]

    if online:
        def step(kt, masked):
            for hh in heads:
                s = logits(kt, hh, masked)
                m_prev = m_sc[hh]
                m_new = jnp.maximum(m_prev, jnp.max(s, axis=0, keepdims=True))
                m_sc[hh] = m_new
                acc_sc[hh] = jnp.exp2(m_prev - m_new) * acc_sc[hh] + _dot(vt_tile(kt, hh), _bf16(jnp.exp2(s - m_new)))

        def body(kt, carry):
            step(kt, False)
            return carry

        lax.fori_loop(0, i, body, 0)
        step(i, True)
    else:
        def stage(kt, rd, wr):
            for hh in heads:
                p_sc[wr, hh] = _bf16(jnp.exp2(logits(kt + 1, hh, False)))
                acc_sc[hh] += _dot(vt_tile(kt, hh), p_sc[rd, hh])

        for hh in heads:
            p0 = _bf16(jnp.exp2(logits(0, hh, True)))
            p_sc[0, hh] = p0
            p_sc[1, hh] = p0

        n_plain = jnp.maximum(i - 1, 0)

        @pl.when(n_plain % 2 == 1)
        def _():
            stage(0, 0, 1)

        def body(j, carry):
            kt = n_plain % 2 + 2 * j
            stage(kt, 1, 0)
            stage(kt + 1, 0, 1)
            return carry

        lax.fori_loop(0, n_plain // 2, body, 0)

        half = tk // 2

        @pl.when(i > 0)
        def _():
            for hh in heads:
                p_sc[0, hh, 0:half, :] = _bf16(jnp.exp2(logits(i, hh, True, 0, half)))
                p_sc[0, hh, half:tk, half:tq] = _bf16(jnp.exp2(logits(i, hh, True, half, tk, half)))
                acc_sc[hh] += _dot(vt_tile(i - 1, hh), p_sc[1, hh])

        for hh in heads:
            vt = vt_tile(i, hh)
            early = _dot(vt[:, 0:half], p_sc[0, hh, 0:half, :])
            late = _dot(vt[:, half:tk], p_sc[0, hh, half:tk, half:tq])
            acc_sc[hh, :, 0:half] += early[:, 0:half]
            acc_sc[hh, :, half:tq] += early[:, half:tq] + late

    for pr in range(N_HEADS // 2):
        o_t = []
        for hh in (2 * pr, 2 * pr + 1):
            acc = acc_sc[hh]
            o_t.append(acc[0:HEAD_DIM, :] / acc[HEAD_DIM:HEAD_DIM + 1, :])
        o_ref[0, :, pr * LANES:(pr + 1) * LANES] = _bf16(jnp.concatenate(o_t, axis=0).T)


def _attn_call(qt, kh, vt, *, online):
    b, s, _ = kh.shape
    tq = Q_TILE
    scratch = [pltpu.VMEM((N_HEADS, VT_ROWS, tq), jnp.float32),
               pltpu.VMEM((N_HEADS, 1, tq), jnp.float32) if online else pltpu.VMEM((2, N_HEADS, K_TILE, tq), jnp.bfloat16)]
    return pl.pallas_call(
        functools.partial(_attn_kernel, online=online),
        grid=(b, s // tq),
        in_specs=[
            pl.BlockSpec((1, 1, N_HEADS * LANES, tq), lambda bi, i: (bi, i, 0, 0)),
            pl.BlockSpec((1, s, N_HEADS * LANES), lambda bi, i: (bi, 0, 0)),
            pl.BlockSpec((1, s // K_TILE, N_HEADS * VT_ROWS, K_TILE), lambda bi, i: (bi, 0, 0, 0)),
        ],
        out_specs=pl.BlockSpec((1, tq, D_ATTN), lambda bi, i: (bi, i, 0)),
        out_shape=jax.ShapeDtypeStruct((b, s, D_ATTN), jnp.bfloat16),
        scratch_shapes=scratch,
        compiler_params=pltpu.CompilerParams(
            dimension_semantics=("arbitrary", "arbitrary"), vmem_limit_bytes=VMEM_LIMIT),
        name="attn_online" if online else "attn",
    )(qt, kh, vt)


def _shift_rows(x, k):
    return pltpu.roll(x, k, axis=0)


def _branch_inputs(cp_ref, halo_ref, cw_ref, seq_tile):
    tm = cp_ref.shape[0]
    halo = halo_ref[...].astype(jnp.float32) * jnp.where(seq_tile == 0, 0.0, 1.0)
    cp = cp_ref[...].astype(jnp.float32)
    z = jnp.concatenate([halo[:, 0:D_CONV], cp[:, 0:D_CONV]], axis=0)
    px = jnp.concatenate([halo[:, 2 * D_CONV:], cp[:, 2 * D_CONV:]], axis=0)
    cb = cp[:, D_CONV:2 * D_CONV]

    cw = cw_ref[...]
    conv = cw[2:3, :] * z + cw[1:2, :] * _shift_rows(z, 1) + cw[0:1, :] * _shift_rows(z, 2)
    u = _bf16(cb * conv[HALO:, :])

    s2 = px + _shift_rows(px, 1)
    s4 = s2 + _shift_rows(s2, 2)
    s8 = s4 + _shift_rows(s4, 4)
    s16 = s8 + _shift_rows(s8, 8)
    grp = lax.broadcasted_iota(jnp.int32, (tm, D_POOL), 1) // POOL_GROUP_DIM
    wsum = jnp.where(grp == 0, s2[HALO:], jnp.where(grp == 1, s4[HALO:], jnp.where(grp == 2, s8[HALO:], s16[HALO:])))
    pos = seq_tile * tm + lax.broadcasted_iota(jnp.int32, (tm, D_POOL), 0)
    win = jnp.left_shift(2, grp)
    counts = jnp.minimum(pos + 1, win).astype(jnp.float32)
    d = _bf16(wsum / counts - px[HALO:])
    return u, d


def _merge_rows(x, a, u, d, g_ref, wg_ref, wao_ref, wco_ref, wpool_ref, ps_ref, wo_ref):
    h = _bf16(x * _rms_scale(x) * g_ref[...])
    y_attn = _dot(a, wao_ref[...])
    y_conv = _dot(u, wco_ref[...])
    y_pool = _dot(d, wpool_ref[...]) * ps_ref[...]
    merged = jax.nn.sigmoid(_dot(h, wg_ref[:, 0:D_MODEL])) * y_attn
    merged += jax.nn.sigmoid(_dot(h, wg_ref[:, D_MODEL:2 * D_MODEL])) * y_conv
    merged += jax.nn.sigmoid(_dot(h, wg_ref[:, 2 * D_MODEL:3 * D_MODEL])) * y_pool
    return x + _dot(_bf16(merged), wo_ref[...])


def _ffn_rows(x, g_ref, wi_ref, wo_ref):
    h = _bf16(x * _rms_scale(x) * g_ref[...])
    acc = x
    for c0, c1 in FF_CHUNKS:
        gt = _dot(h, wi_ref[:, c0:c1])
        up = _dot(h, wi_ref[:, D_FF + c0:D_FF + c1])
        act = _bf16(gt * jax.nn.sigmoid(gt) * up)
        acc = acc + _dot(act, wo_ref[c0:c1, :])
    return acc


def _mix_ffn_kernel(x_ref, a_ref, cp_ref, halo_ref, g_ref, wg_ref, wao_ref, cw_ref, wco_ref, wpool_ref, ps_ref, wo_ref,
                    g2_ref, wi_ref, wfo_ref, o_ref, *, tiles_per_seq):
    seq_tile = pl.program_id(0) % tiles_per_seq
    u, d = _branch_inputs(cp_ref, halo_ref, cw_ref, seq_tile)
    for r0 in range(0, x_ref.shape[0], SUB_ROWS):
        rows = slice(r0, r0 + SUB_ROWS)
        x1 = _merge_rows(x_ref[rows, :], a_ref[rows, :], u[rows], d[rows], g_ref, wg_ref, wao_ref, wco_ref, wpool_ref,
                         ps_ref, wo_ref)
        o_ref[rows, :] = _ffn_rows(x1, g2_ref, wi_ref, wfo_ref)


def _mix_ffn_call(layer, x2, a2, cp, g, wg, wao, cw, wco, wpool, ps, wo, g2, wi, wfo, *, seq):
    t = x2.shape[0]
    tm = ROW_TILE
    row = lambda i: (i, 0)
    cpw = 2 * D_CONV + D_POOL
    halo_blocks = tm // HALO
    return pl.pallas_call(
        functools.partial(_mix_ffn_kernel, tiles_per_seq=seq // tm),
        grid=(t // tm,),
        in_specs=[
            pl.BlockSpec((tm, D_MODEL), row),
            pl.BlockSpec((tm, D_ATTN), row),
            pl.BlockSpec((tm, cpw), row),
            pl.BlockSpec((HALO, cpw), lambda i: (jnp.maximum(i * halo_blocks - 1, 0), 0)),
            _layer_spec(layer, (1, D_MODEL)),
            _layer_spec(layer, (D_MODEL, 3 * D_MODEL)),
            _layer_spec(layer, (D_ATTN, D_MODEL)),
            _layer_spec(layer, (8, D_CONV)),
            _layer_spec(layer, (D_CONV, D_MODEL)),
            _layer_spec(layer, (D_POOL, D_MODEL)),
            _layer_spec(layer, (1, D_MODEL)),
            _layer_spec(layer, (D_MODEL, D_MODEL)),
            _layer_spec(layer, (1, D_MODEL)),
            _layer_spec(layer, (D_MODEL, 2 * D_FF)),
            _layer_spec(layer, (D_FF, D_MODEL)),
        ],
        out_specs=pl.BlockSpec((tm, D_MODEL), row),
        out_shape=jax.ShapeDtypeStruct((t, D_MODEL), jnp.float32),
        compiler_params=pltpu.CompilerParams(dimension_semantics=("arbitrary",), vmem_limit_bytes=VMEM_LIMIT),
        name="mix_ffn",
    )(x2, a2, cp, cp, g, wg, wao, cw, wco, wpool, ps, wo, g2, wi, wfo)


def _placement_matrices():
    pq = np.zeros((N_PIECES * LANES, LANES), np.float32)
    pk = np.zeros((N_PIECES * LANES, LANES), np.float32)
    ones_row = N_HEADS
    for h in range(N_HEADS):
        for j in range(N_PIECES):
            pq[j * LANES + h, N_PIECES * h + j] = 1.0
            pk[ones_row, N_PIECES * h + j] = 1.0
            pq[ones_row, X_ONES + N_PIECES * h + j] = 1.0
            pk[j * LANES + h, X_ONES + N_PIECES * h + j] = -1.0
    return jnp.asarray(np.concatenate([pq, pk], axis=1), jnp.bfloat16)


def kernel(x, norm_mix_g, w_in, forget_b, q_norm_g, k_norm_g, w_attn_out, conv_w, w_conv_out, pool_w, pool_scale,
           w_o, norm_ffn_g, w_ffn_in, w_ffn_out):
    bsz, seq, _ = x.shape
    depth = w_in.shape[0]
    assert seq % ROW_TILE == 0 and seq % PROJ_TILE == 0 and PROJ_TILE % K_TILE == 0 and Q_TILE == K_TILE

    bf = jnp.bfloat16
    eh = jnp.asarray(np.kron(np.eye(N_HEADS), np.ones((HEAD_DIM, HEAD_DIM))), bf)
    tri = jnp.asarray(np.tril(np.ones((K_TILE, K_TILE))), bf)
    pqk = _placement_matrices()

    row3 = lambda p: p[:, None, :]
    w_qkv = w_in[:, :, :IN_F].astype(bf)
    w_f = jnp.pad(w_in[:, :, IN_F:IN_C], ((0, 0), (0, 0), (0, LANES - N_HEADS))).astype(bf)
    w_c = w_in[:, :, IN_C:IN_G].astype(bf)
    w_g = w_in[:, :, IN_G:].astype(bf)
    fb = row3(jnp.pad(forget_b, ((0, 0), (0, LANES - N_HEADS))))
    q_scale = (HEAD_DIM ** -0.5) * LOG2E
    gq = row3(jnp.tile(q_norm_g, (1, N_HEADS))) * q_scale
    gk = row3(jnp.tile(k_norm_g, (1, N_HEADS)))
    g_mix, g_ffn, p_scale = row3(norm_mix_g), row3(norm_ffn_g), row3(pool_scale)
    wao = w_attn_out.astype(bf)
    cw = jnp.pad(conv_w, ((0, 0), (0, 8 - CONV_K), (0, 0)))
    wco = w_conv_out.astype(bf)
    grp_mask = jnp.asarray(np.kron(np.eye(len(POOL_WINDOWS)), np.ones((POOL_GROUP_DIM, POOL_OUT_DIM))), jnp.float32)
    wpool = (jnp.tile(pool_w.reshape(depth, D_POOL, POOL_OUT_DIM), (1, 1, len(POOL_WINDOWS))) * grp_mask).astype(bf)
    wo = w_o.astype(bf)
    wfi = w_ffn_in.astype(bf)
    wfo = w_ffn_out.astype(bf)

    x2 = x.reshape(bsz * seq, D_MODEL)
    for l in range(depth):
        qt, kh, vt, cp = _proj_call(l, x2, g_mix, w_qkv, w_f, w_c, fb, gq, gk, eh, tri, pqk, seq=seq)
        logit_bound = HEAD_DIM * jnp.max(jnp.abs(gq[l])) * jnp.max(jnp.abs(gk[l]))
        a = lax.cond(logit_bound < UNSHIFTED_LOGIT_LIMIT,
                     functools.partial(_attn_call, online=False), functools.partial(_attn_call, online=True),
                     qt, kh.reshape(bsz, seq, N_HEADS * LANES), vt)
        x2 = _mix_ffn_call(l, x2, a.reshape(bsz * seq, D_ATTN), cp, g_mix, w_g, wao, cw, wco, wpool, p_scale, wo,
                           g_ffn, wfi, wfo, seq=seq)
    return x2.reshape(bsz, seq, D_MODEL)
```

```python
import functools
import math

import numpy as np
import jax
import jax.numpy as jnp
from jax import lax
from jax.experimental import pallas as pl
from jax.experimental.pallas import tpu as pltpu

D_MODEL = 1024
HEAD_DIM = 64
D_ATTN = D_MODEL // 2
N_HEADS = D_ATTN // HEAD_DIM
D_CONV = D_MODEL // 4
CONV_K = 3
D_POOL = D_MODEL // 4
POOL_WINDOWS = (2, 4, 8, 16)
POOL_GROUP_DIM = D_POOL // len(POOL_WINDOWS)
POOL_OUT_DIM = D_MODEL // len(POOL_WINDOWS)
D_FF = -(-8 * D_MODEL // (3 * 256)) * 256
EPS = 1e-6

LANES = 128
VT_ROWS = HEAD_DIM + 16
HALO = 16
N_PIECES = 3
LOG2E = 1.4426950408889634
NEG_BIG = -1e30
UNSHIFTED_LOGIT_LIMIT = 100.0

ROW_TILE = 1024
PROJ_TILE = 1024
SUB_ROWS = 256
Q_TILE = 512
K_TILE = 512
FF_CHUNKS = ((0, 1024), (1024, 2048), (2048, D_FF))
VMEM_LIMIT = 56 * 1024 * 1024

IN_F = 3 * D_ATTN
IN_C = IN_F + N_HEADS
IN_G = IN_C + 3 * D_CONV + D_POOL
C_CX, C_CB, C_CC, C_PX = 0, D_CONV, 2 * D_CONV, 3 * D_CONV

X_ONES = N_HEADS * N_PIECES


def _bf16(x):
    return x.astype(jnp.bfloat16)


def _dot(a, b):
    return jnp.dot(a, b, preferred_element_type=jnp.float32)


def _split3(x):
    hi = _bf16(x)
    r1 = x - hi.astype(jnp.float32)
    mid = _bf16(r1)
    lo = _bf16(r1 - mid.astype(jnp.float32))
    return hi, mid, lo


def _rms_scale(x):
    return lax.rsqrt(jnp.mean(x * x, axis=-1, keepdims=True) + EPS)


def _proj_kernel(x_ref, g_ref, wqkv_ref, wf_ref, wc_ref, fb_ref, gq_ref, gk_ref, eh_ref, tri_ref, pqk_ref,
                 qt_ref, kh_ref, vt_ref, cp_ref, carry_ref, *, tiles_per_seq):
    i = pl.program_id(0)

    @pl.when(i % tiles_per_seq == 0)
    def _():
        carry_ref[...] = jnp.zeros_like(carry_ref)

    sub = K_TILE
    lane = lax.broadcasted_iota(jnp.int32, (sub, LANES), 1)
    ex_keep = ((lane >= HEAD_DIM) & (lane < HEAD_DIM + N_PIECES)) \
        | ((lane >= HEAD_DIM + X_ONES) & (lane < HEAD_DIM + X_ONES + N_PIECES))
    ones_rows = _bf16(jnp.where(lax.broadcasted_iota(jnp.int32, (VT_ROWS - HEAD_DIM, sub), 0) == 0, 1.0, 0.0))
    carry = carry_ref[...]

    for j in range(x_ref.shape[0] // sub):
        rows = slice(j * sub, (j + 1) * sub)
        x = x_ref[rows, :]
        h = _bf16(x * _rms_scale(x) * g_ref[...])

        f = _dot(h, wf_ref[...]) + fb_ref[...]
        logf = jnp.minimum(f, 0.0) - jnp.log1p(jnp.exp(-jnp.abs(f)))
        logf = jnp.where(lane < N_HEADS, logf, 0.0)
        pieces = jnp.concatenate(_split3(logf)[:2], axis=1)
        cum = _dot(tri_ref[...], pieces)
        c = cum[:, 0:LANES] + cum[:, LANES:2 * LANES] + carry
        carry = c[sub - 1:, :]

        ap = jnp.concatenate(_split3(jnp.where(lane == N_HEADS, 1.0, c * LOG2E)), axis=1)
        ex = _dot(ap, pqk_ref[...])
        exq, exk = ex[:, 0:LANES], ex[:, LANES:2 * LANES]

        q = _dot(h, wqkv_ref[:, 0:D_ATTN])
        k = _dot(h, wqkv_ref[:, D_ATTN:2 * D_ATTN])
        ssq = _dot(_bf16(q * q), eh_ref[...])
        ssk = _dot(_bf16(k * k), eh_ref[...])
        qn = q * lax.rsqrt(ssq * (1.0 / HEAD_DIM) + EPS) * gq_ref[...]
        kn = k * lax.rsqrt(ssk * (1.0 / HEAD_DIM) + EPS) * gk_ref[...]

        for hd in range(N_HEADS):
            sl = slice((hd // 2) * LANES, (hd // 2 + 1) * LANES)
            blk_q, blk_k = qn[:, sl], kn[:, sl]
            if hd % 2:
                blk_q = pltpu.roll(blk_q, HEAD_DIM, axis=1)
                blk_k = pltpu.roll(blk_k, HEAD_DIM, axis=1)
            shift = HEAD_DIM - N_PIECES * hd
            qh = jnp.where(lane < HEAD_DIM, blk_q, jnp.where(ex_keep, pltpu.roll(exq, shift, axis=1), 0.0))
            kh = jnp.where(lane < HEAD_DIM, blk_k, jnp.where(ex_keep, pltpu.roll(exk, shift, axis=1), 0.0))
            qt_ref[0, j, hd * LANES:(hd + 1) * LANES, :] = _bf16(qh.T)
            kh_ref[rows, hd * LANES:(hd + 1) * LANES] = _bf16(kh)

        v = _dot(h, wqkv_ref[:, 2 * D_ATTN:3 * D_ATTN])
        for pr in range(N_HEADS // 2):
            vt = _bf16(v[:, pr * LANES:(pr + 1) * LANES].T)
            for hh in range(2):
                base = (2 * pr + hh) * VT_ROWS
                vt_ref[0, j, base:base + HEAD_DIM, :] = vt[hh * HEAD_DIM:(hh + 1) * HEAD_DIM, :]
                vt_ref[0, j, base + HEAD_DIM:base + VT_ROWS, :] = ones_rows

        r = _dot(h, wc_ref[...])
        cp_ref[rows, 0:D_CONV] = _bf16(r[:, C_CC:C_CC + D_CONV] * r[:, C_CX:C_CX + D_CONV])
        cp_ref[rows, D_CONV:2 * D_CONV] = _bf16(r[:, C_CB:C_CB + D_CONV])
        cp_ref[rows, 2 * D_CONV:2 * D_CONV + D_POOL] = _bf16(r[:, C_PX:C_PX + D_POOL])

    carry_ref[...] = carry


def _const_spec(shape):
    return pl.BlockSpec(shape, lambda *_: (0,) * len(shape), pipeline_mode=pl.Buffered(1))


def _layer_spec(layer, shape):
    return pl.BlockSpec((None,) + shape, lambda *_: (layer,) + (0,) * len(shape), pipeline_mode=pl.Buffered(1))


def _proj_call(layer, x2, g, wqkv, wf, wc, fb, gq, gk, eh, tri, pqk, *, seq):
    t = x2.shape[0]
    tm, kt = PROJ_TILE, K_TILE
    row = lambda i: (i, 0)
    outs = [
        jax.ShapeDtypeStruct((t // seq, seq // kt, N_HEADS * LANES, kt), jnp.bfloat16),
        jax.ShapeDtypeStruct((t, N_HEADS * LANES), jnp.bfloat16),
        jax.ShapeDtypeStruct((t // seq, seq // kt, N_HEADS * VT_ROWS, kt), jnp.bfloat16),
        jax.ShapeDtypeStruct((t, 2 * D_CONV + D_POOL), jnp.bfloat16),
    ]
    tile_t = lambda i: (i // (seq // tm), i % (seq // tm), 0, 0)
    return pl.pallas_call(
        functools.partial(_proj_kernel, tiles_per_seq=seq // tm),
        grid=(t // tm,),
        in_specs=[
            pl.BlockSpec((tm, D_MODEL), row),
            _layer_spec(layer, (1, D_MODEL)),
            _layer_spec(layer, (D_MODEL, 3 * D_ATTN)),
            _layer_spec(layer, (D_MODEL, LANES)),
            _layer_spec(layer, (D_MODEL, 3 * D_CONV + D_POOL)),
            _layer_spec(layer, (1, LANES)),
            _layer_spec(layer, (1, D_ATTN)),
            _layer_spec(layer, (1, D_ATTN)),
            _const_spec((D_ATTN, D_ATTN)),
            _const_spec((kt, kt)),
            _const_spec((N_PIECES * LANES, 2 * LANES)),
        ],
        out_specs=[
            pl.BlockSpec((1, tm // kt, N_HEADS * LANES, kt), tile_t),
            pl.BlockSpec((tm, N_HEADS * LANES), row),
            pl.BlockSpec((1, tm // kt, N_HEADS * VT_ROWS, kt), tile_t),
            pl.BlockSpec((tm, 2 * D_CONV + D_POOL), row),
        ],
        out_shape=outs,
        scratch_shapes=[pltpu.VMEM((1, LANES), jnp.float32)],
        compiler_params=pltpu.CompilerParams(dimension_semantics=("arbitrary",), vmem_limit_bytes=VMEM_LIMIT),
        name="proj",
    )(x2, g, wqkv, wf, wc, fb, gq, gk, eh, tri, pqk)


def _attn_kernel(qt_ref, kh_ref, vt_ref, o_ref, acc_sc, aux_sc, *, online):
    i = pl.program_id(1)
    tq, tk = Q_TILE, K_TILE
    heads = range(N_HEADS)

    acc_sc[...] = jnp.zeros_like(acc_sc)
    m_sc = p_sc = aux_sc
    if online:
        m_sc[...] = jnp.full_like(m_sc, NEG_BIG)

    def logits(kt, hh, masked, k0=0, k1=tk, q0=0):
        start = pl.multiple_of(kt * tk + k0, math.gcd(tk, k0))
        s = _dot(kh_ref[0, pl.ds(start, k1 - k0), hh * LANES:(hh + 1) * LANES],
                 qt_ref[0, 0, hh * LANES:(hh + 1) * LANES, q0:])
        if masked:
            k_id = lax.broadcasted_iota(jnp.int32, s.shape, 0) + (kt - i) * tk + k0
            q_id = lax.broadcasted_iota(jnp.int32, s.shape, 1) + q0
            s = jnp.where(k_id <= q_id, s, NEG_BIG)
        return s

    def vt_tile(kt, hh):
        return vt_ref[0, kt, hh * VT_ROWS:(hh + 1) * VT_ROWS, :]

    if online:
        def step(kt, masked):
            for hh in heads:
                s = logits(kt, hh, masked)
                m_prev = m_sc[hh]
                m_new = jnp.maximum(m_prev, jnp.max(s, axis=0, keepdims=True))
                m_sc[hh] = m_new
                acc_sc[hh] = jnp.exp2(m_prev - m_new) * acc_sc[hh] + _dot(vt_tile(kt, hh), _bf16(jnp.exp2(s - m_new)))

        def body(kt, carry):
            step(kt, False)
            return carry

        lax.fori_loop(0, i, body, 0)
        step(i, True)
    else:
        def stage(kt, rd, wr):
            for hh in heads:
                p_sc[wr, hh] = _bf16(jnp.exp2(logits(kt + 1, hh, False)))
                acc_sc[hh] += _dot(vt_tile(kt, hh), p_sc[rd, hh])

        for hh in heads:
            p0 = _bf16(jnp.exp2(logits(0, hh, True)))
            p_sc[0, hh] = p0
            p_sc[1, hh] = p0

        n_plain = jnp.maximum(i - 1, 0)

        @pl.when(n_plain % 2 == 1)
        def _():
            stage(0, 0, 1)

        def body(j, carry):
            kt = n_plain % 2 + 2 * j
            stage(kt, 1, 0)
            stage(kt + 1, 0, 1)
            return carry

        lax.fori_loop(0, n_plain // 2, body, 0)

        half = tk // 2

        @pl.when(i > 0)
        def _():
            for hh in heads:
                p_sc[0, hh, 0:half, :] = _bf16(jnp.exp2(logits(i, hh, True, 0, half)))
                p_sc[0, hh, half:tk, half:tq] = _bf16(jnp.exp2(logits(i, hh, True, half, tk, half)))
                acc_sc[hh] += _dot(vt_tile(i - 1, hh), p_sc[1, hh])

        for hh in heads:
            vt = vt_tile(i, hh)
            early = _dot(vt[:, 0:half], p_sc[0, hh, 0:half, :])
            late = _dot(vt[:, half:tk], p_sc[0, hh, half:tk, half:tq])
            acc_sc[hh, :, 0:half] += early[:, 0:half]
            acc_sc[hh, :, half:tq] += early[:, half:tq] + late

    for pr in range(N_HEADS // 2):
        o_t = []
        for hh in (2 * pr, 2 * pr + 1):
            acc = acc_sc[hh]
            o_t.append(acc[0:HEAD_DIM, :] / acc[HEAD_DIM:HEAD_DIM + 1, :])
        o_ref[0, :, pr * LANES:(pr + 1) * LANES] = _bf16(jnp.concatenate(o_t, axis=0).T)


def _attn_call(qt, kh, vt, *, online):
    b, s, _ = kh.shape
    tq = Q_TILE
    scratch = [pltpu.VMEM((N_HEADS, VT_ROWS, tq), jnp.float32),
               pltpu.VMEM((N_HEADS, 1, tq), jnp.float32) if online else pltpu.VMEM((2, N_HEADS, K_TILE, tq), jnp.bfloat16)]
    return pl.pallas_call(
        functools.partial(_attn_kernel, online=online),
        grid=(b, s // tq),
        in_specs=[
            pl.BlockSpec((1, 1, N_HEADS * LANES, tq), lambda bi, i: (bi, i, 0, 0)),
            pl.BlockSpec((1, s, N_HEADS * LANES), lambda bi, i: (bi, 0, 0)),
            pl.BlockSpec((1, s // K_TILE, N_HEADS * VT_ROWS, K_TILE), lambda bi, i: (bi, 0, 0, 0)),
        ],
        out_specs=pl.BlockSpec((1, tq, D_ATTN), lambda bi, i: (bi, i, 0)),
        out_shape=jax.ShapeDtypeStruct((b, s, D_ATTN), jnp.bfloat16),
        scratch_shapes=scratch,
        compiler_params=pltpu.CompilerParams(
            dimension_semantics=("arbitrary", "arbitrary"), vmem_limit_bytes=VMEM_LIMIT),
        name="attn_online" if online else "attn",
    )(qt, kh, vt)


def _shift_rows(x, k):
    return pltpu.roll(x, k, axis=0)


def _branch_inputs(cp_ref, halo_ref, cw_ref, seq_tile):
    tm = cp_ref.shape[0]
    halo = halo_ref[...].astype(jnp.float32) * jnp.where(seq_tile == 0, 0.0, 1.0)
    cp = cp_ref[...].astype(jnp.float32)
    z = jnp.concatenate([halo[:, 0:D_CONV], cp[:, 0:D_CONV]], axis=0)
    px = jnp.concatenate([halo[:, 2 * D_CONV:], cp[:, 2 * D_CONV:]], axis=0)
    cb = cp[:, D_CONV:2 * D_CONV]

    cw = cw_ref[...]
    conv = cw[2:3, :] * z + cw[1:2, :] * _shift_rows(z, 1) + cw[0:1, :] * _shift_rows(z, 2)
    u = _bf16(cb * conv[HALO:, :])

    s2 = px + _shift_rows(px, 1)
    s4 = s2 + _shift_rows(s2, 2)
    s8 = s4 + _shift_rows(s4, 4)
    s16 = s8 + _shift_rows(s8, 8)
    grp = lax.broadcasted_iota(jnp.int32, (tm, D_POOL), 1) // POOL_GROUP_DIM
    wsum = jnp.where(grp == 0, s2[HALO:], jnp.where(grp == 1, s4[HALO:], jnp.where(grp == 2, s8[HALO:], s16[HALO:])))
    pos = seq_tile * tm + lax.broadcasted_iota(jnp.int32, (tm, D_POOL), 0)
    win = jnp.left_shift(2, grp)
    counts = jnp.minimum(pos + 1, win).astype(jnp.float32)
    d = _bf16(wsum / counts - px[HALO:])
    return u, d


def _merge_rows(x, a, u, d, g_ref, wg_ref, wao_ref, wco_ref, wpool_ref, ps_ref, wo_ref):
    h = _bf16(x * _rms_scale(x) * g_ref[...])
    y_attn = _dot(a, wao_ref[...])
    y_conv = _dot(u, wco_ref[...])
    y_pool = _dot(d, wpool_ref[...]) * ps_ref[...]
    merged = jax.nn.sigmoid(_dot(h, wg_ref[:, 0:D_MODEL])) * y_attn
    merged += jax.nn.sigmoid(_dot(h, wg_ref[:, D_MODEL:2 * D_MODEL])) * y_conv
    merged += jax.nn.sigmoid(_dot(h, wg_ref[:, 2 * D_MODEL:3 * D_MODEL])) * y_pool
    return x + _dot(_bf16(merged), wo_ref[...])


def _ffn_rows(x, g_ref, wi_ref, wo_ref):
    h = _bf16(x * _rms_scale(x) * g_ref[...])
    acc = x
    for c0, c1 in FF_CHUNKS:
        gt = _dot(h, wi_ref[:, c0:c1])
        up = _dot(h, wi_ref[:, D_FF + c0:D_FF + c1])
        act = _bf16(gt * jax.nn.sigmoid(gt) * up)
        acc = acc + _dot(act, wo_ref[c0:c1, :])
    return acc


def _mix_ffn_kernel(x_ref, a_ref, cp_ref, halo_ref, g_ref, wg_ref, wao_ref, cw_ref, wco_ref, wpool_ref, ps_ref, wo_ref,
                    g2_ref, wi_ref, wfo_ref, o_ref, *, tiles_per_seq):
    seq_tile = pl.program_id(0) % tiles_per_seq
    u, d = _branch_inputs(cp_ref, halo_ref, cw_ref, seq_tile)
    for r0 in range(0, x_ref.shape[0], SUB_ROWS):
        rows = slice(r0, r0 + SUB_ROWS)
        x1 = _merge_rows(x_ref[rows, :], a_ref[rows, :], u[rows], d[rows], g_ref, wg_ref, wao_ref, wco_ref, wpool_ref,
                         ps_ref, wo_ref)
        o_ref[rows, :] = _ffn_rows(x1, g2_ref, wi_ref, wfo_ref)


def _mix_ffn_call(layer, x2, a2, cp, g, wg, wao, cw, wco, wpool, ps, wo, g2, wi, wfo, *, seq):
    t = x2.shape[0]
    tm = ROW_TILE
    row = lambda i: (i, 0)
    cpw = 2 * D_CONV + D_POOL
    halo_blocks = tm // HALO
    return pl.pallas_call(
        functools.partial(_mix_ffn_kernel, tiles_per_seq=seq // tm),
        grid=(t // tm,),
        in_specs=[
            pl.BlockSpec((tm, D_MODEL), row),
            pl.BlockSpec((tm, D_ATTN), row),
            pl.BlockSpec((tm, cpw), row),
            pl.BlockSpec((HALO, cpw), lambda i: (jnp.maximum(i * halo_blocks - 1, 0), 0)),
            _layer_spec(layer, (1, D_MODEL)),
            _layer_spec(layer, (D_MODEL, 3 * D_MODEL)),
            _layer_spec(layer, (D_ATTN, D_MODEL)),
            _layer_spec(layer, (8, D_CONV)),
            _layer_spec(layer, (D_CONV, D_MODEL)),
            _layer_spec(layer, (D_POOL, D_MODEL)),
            _layer_spec(layer, (1, D_MODEL)),
            _layer_spec(layer, (D_MODEL, D_MODEL)),
            _layer_spec(layer, (1, D_MODEL)),
            _layer_spec(layer, (D_MODEL, 2 * D_FF)),
            _layer_spec(layer, (D_FF, D_MODEL)),
        ],
        out_specs=pl.BlockSpec((tm, D_MODEL), row),
        out_shape=jax.ShapeDtypeStruct((t, D_MODEL), jnp.float32),
        compiler_params=pltpu.CompilerParams(dimension_semantics=("arbitrary",), vmem_limit_bytes=VMEM_LIMIT),
        name="mix_ffn",
    )(x2, a2, cp, cp, g, wg, wao, cw, wco, wpool, ps, wo, g2, wi, wfo)


def _placement_matrices():
    pq = np.zeros((N_PIECES * LANES, LANES), np.float32)
    pk = np.zeros((N_PIECES * LANES, LANES), np.float32)
    ones_row = N_HEADS
    for h in range(N_HEADS):
        for j in range(N_PIECES):
            pq[j * LANES + h, N_PIECES * h + j] = 1.0
            pk[ones_row, N_PIECES * h + j] = 1.0
            pq[ones_row, X_ONES + N_PIECES * h + j] = 1.0
            pk[j * LANES + h, X_ONES + N_PIECES * h + j] = -1.0
    return jnp.asarray(np.concatenate([pq, pk], axis=1), jnp.bfloat16)


def kernel(x, norm_mix_g, w_in, forget_b, q_norm_g, k_norm_g, w_attn_out, conv_w, w_conv_out, pool_w, pool_scale,
           w_o, norm_ffn_g, w_ffn_in, w_ffn_out):
    bsz, seq, _ = x.shape
    depth = w_in.shape[0]
    assert seq % ROW_TILE == 0 and seq % PROJ_TILE == 0 and PROJ_TILE % K_TILE == 0 and Q_TILE == K_TILE

    bf = jnp.bfloat16
    eh = jnp.asarray(np.kron(np.eye(N_HEADS), np.ones((HEAD_DIM, HEAD_DIM))), bf)
    tri = jnp.asarray(np.tril(np.ones((K_TILE, K_TILE))), bf)
    pqk = _placement_matrices()

    row3 = lambda p: p[:, None, :]
    w_qkv = w_in[:, :, :IN_F].astype(bf)
    w_f = jnp.pad(w_in[:, :, IN_F:IN_C], ((0, 0), (0, 0), (0, LANES - N_HEADS))).astype(bf)
    w_c = w_in[:, :, IN_C:IN_G].astype(bf)
    w_g = w_in[:, :, IN_G:].astype(bf)
    fb = row3(jnp.pad(forget_b, ((0, 0), (0, LANES - N_HEADS))))
    q_scale = (HEAD_DIM ** -0.5) * LOG2E
    gq = row3(jnp.tile(q_norm_g, (1, N_HEADS))) * q_scale
    gk = row3(jnp.tile(k_norm_g, (1, N_HEADS)))
    g_mix, g_ffn, p_scale = row3(norm_mix_g), row3(norm_ffn_g), row3(pool_scale)
    wao = w_attn_out.astype(bf)
    cw = jnp.pad(conv_w, ((0, 0), (0, 8 - CONV_K), (0, 0)))
    wco = w_conv_out.astype(bf)
    grp_mask = jnp.asarray(np.kron(np.eye(len(POOL_WINDOWS)), np.ones((POOL_GROUP_DIM, POOL_OUT_DIM))), jnp.float32)
    wpool = (jnp.tile(pool_w.reshape(depth, D_POOL, POOL_OUT_DIM), (1, 1, len(POOL_WINDOWS))) * grp_mask).astype(bf)
    wo = w_o.astype(bf)
    wfi = w_ffn_in.astype(bf)
    wfo = w_ffn_out.astype(bf)

    x2 = x.reshape(bsz * seq, D_MODEL)
    for l in range(depth):
        qt, kh, vt, cp = _proj_call(l, x2, g_mix, w_qkv, w_f, w_c, fb, gq, gk, eh, tri, pqk, seq=seq)
        logit_bound = HEAD_DIM * jnp.max(jnp.abs(gq[l])) * jnp.max(jnp.abs(gk[l]))
        a = lax.cond(logit_bound < UNSHIFTED_LOGIT_LIMIT,
                     functools.partial(_attn_call, online=False), functools.partial(_attn_call, online=True),
                     qt, kh.reshape(bsz, seq, N_HEADS * LANES), vt)
        x2 = _mix_ffn_call(l, x2, a.reshape(bsz * seq, D_ATTN), cp, g_mix, w_g, wao, cw, wco, wpool, p_scale, wo,
                           g_ffn, wfi, wfo, seq=seq)
    return x2.reshape(bsz, seq, D_MODEL)
```

```python
import functools
import math

import numpy as np
import jax
import jax.numpy as jnp
from jax import lax
from jax.experimental import pallas as pl
from jax.experimental.pallas import tpu as pltpu

D_MODEL = 1024
HEAD_DIM = 64
D_ATTN = D_MODEL // 2
N_HEADS = D_ATTN // HEAD_DIM
D_CONV = D_MODEL // 4
CONV_K = 3
D_POOL = D_MODEL // 4
POOL_WINDOWS = (2, 4, 8, 16)
POOL_GROUP_DIM = D_POOL // len(POOL_WINDOWS)
POOL_OUT_DIM = D_MODEL // len(POOL_WINDOWS)
D_FF = -(-8 * D_MODEL // (3 * 256)) * 256
EPS = 1e-6

LANES = 128
VT_ROWS = HEAD_DIM + 16
HALO = 16
N_PIECES = 3
LOG2E = 1.4426950408889634
NEG_BIG = -1e30
UNSHIFTED_LOGIT_LIMIT = 100.0

ROW_TILE = 512
PROJ_TILE = 1024
SUB_ROWS = 256
Q_TILE = 512
K_TILE = 512
FF_CHUNKS = ((0, 1024), (1024, 2048), (2048, D_FF))
VMEM_LIMIT = 56 * 1024 * 1024

IN_F = 3 * D_ATTN
IN_C = IN_F + N_HEADS
IN_G = IN_C + 3 * D_CONV + D_POOL
C_CX, C_CB, C_CC, C_PX = 0, D_CONV, 2 * D_CONV, 3 * D_CONV

X_ONES = N_HEADS * N_PIECES


def _bf16(x):
    return x.astype(jnp.bfloat16)


def _dot(a, b):
    return jnp.dot(a, b, preferred_element_type=jnp.float32)


def _split3(x):
    hi = _bf16(x)
    r1 = x - hi.astype(jnp.float32)
    mid = _bf16(r1)
    lo = _bf16(r1 - mid.astype(jnp.float32))
    return hi, mid, lo


def _rms_scale(x):
    return lax.rsqrt(jnp.mean(x * x, axis=-1, keepdims=True) + EPS)


def _proj_kernel(x_ref, g_ref, wqkv_ref, wf_ref, wc_ref, fb_ref, gq_ref, gk_ref, eh_ref, tri_ref, pqk_ref,
                 qt_ref, kh_ref, vt_ref, cp_ref, carry_ref, *, tiles_per_seq):
    i = pl.program_id(0)

    @pl.when(i % tiles_per_seq == 0)
    def _():
        carry_ref[...] = jnp.zeros_like(carry_ref)

    sub = K_TILE
    lane = lax.broadcasted_iota(jnp.int32, (sub, LANES), 1)
    ex_keep = ((lane >= HEAD_DIM) & (lane < HEAD_DIM + N_PIECES)) \
        | ((lane >= HEAD_DIM + X_ONES) & (lane < HEAD_DIM + X_ONES + N_PIECES))
    ones_rows = _bf16(jnp.where(lax.broadcasted_iota(jnp.int32, (VT_ROWS - HEAD_DIM, sub), 0) == 0, 1.0, 0.0))
    carry = carry_ref[...]

    for j in range(x_ref.shape[0] // sub):
        rows = slice(j * sub, (j + 1) * sub)
        x = x_ref[rows, :]
        h = _bf16(x * _rms_scale(x) * g_ref[...])

        f = _dot(h, wf_ref[...]) + fb_ref[...]
        logf = jnp.minimum(f, 0.0) - jnp.log1p(jnp.exp(-jnp.abs(f)))
        logf = jnp.where(lane < N_HEADS, logf, 0.0)
        pieces = jnp.concatenate(_split3(logf)[:2], axis=1)
        cum = _dot(tri_ref[...], pieces)
        c = cum[:, 0:LANES] + cum[:, LANES:2 * LANES] + carry
        carry = c[sub - 1:, :]

        ap = jnp.concatenate(_split3(jnp.where(lane == N_HEADS, 1.0, c * LOG2E)), axis=1)
        ex = _dot(ap, pqk_ref[...])
        exq, exk = ex[:, 0:LANES], ex[:, LANES:2 * LANES]

        q = _dot(h, wqkv_ref[:, 0:D_ATTN])
        k = _dot(h, wqkv_ref[:, D_ATTN:2 * D_ATTN])
        ssq = _dot(_bf16(q * q), eh_ref[...])
        ssk = _dot(_bf16(k * k), eh_ref[...])
        qn = q * lax.rsqrt(ssq * (1.0 / HEAD_DIM) + EPS) * gq_ref[...]
        kn = k * lax.rsqrt(ssk * (1.0 / HEAD_DIM) + EPS) * gk_ref[...]

        for hd in range(N_HEADS):
            sl = slice((hd // 2) * LANES, (hd // 2 + 1) * LANES)
            blk_q, blk_k = qn[:, sl], kn[:, sl]
            if hd % 2:
                blk_q = pltpu.roll(blk_q, HEAD_DIM, axis=1)
                blk_k = pltpu.roll(blk_k, HEAD_DIM, axis=1)
            shift = HEAD_DIM - N_PIECES * hd
            qh = jnp.where(lane < HEAD_DIM, blk_q, jnp.where(ex_keep, pltpu.roll(exq, shift, axis=1), 0.0))
            kh = jnp.where(lane < HEAD_DIM, blk_k, jnp.where(ex_keep, pltpu.roll(exk, shift, axis=1), 0.0))
            qt_ref[0, j, hd * LANES:(hd + 1) * LANES, :] = _bf16(qh.T)
            kh_ref[rows, hd * LANES:(hd + 1) * LANES] = _bf16(kh)

        v = _dot(h, wqkv_ref[:, 2 * D_ATTN:3 * D_ATTN])
        for pr in range(N_HEADS // 2):
            vt = _bf16(v[:, pr * LANES:(pr + 1) * LANES].T)
            for hh in range(2):
                base = (2 * pr + hh) * VT_ROWS
                vt_ref[0, j, base:base + HEAD_DIM, :] = vt[hh * HEAD_DIM:(hh + 1) * HEAD_DIM, :]
                vt_ref[0, j, base + HEAD_DIM:base + VT_ROWS, :] = ones_rows

        r = _dot(h, wc_ref[...])
        cp_ref[rows, 0:D_CONV] = _bf16(r[:, C_CC:C_CC + D_CONV] * r[:, C_CX:C_CX + D_CONV])
        cp_ref[rows, D_CONV:2 * D_CONV] = _bf16(r[:, C_CB:C_CB + D_CONV])
        cp_ref[rows, 2 * D_CONV:2 * D_CONV + D_POOL] = _bf16(r[:, C_PX:C_PX + D_POOL])

    carry_ref[...] = carry


def _const_spec(shape):
    return pl.BlockSpec(shape, lambda *_: (0,) * len(shape), pipeline_mode=pl.Buffered(1))


def _layer_spec(layer, shape):
    return pl.BlockSpec((None,) + shape, lambda *_: (layer,) + (0,) * len(shape), pipeline_mode=pl.Buffered(1))


def _proj_call(layer, x2, g, wqkv, wf, wc, fb, gq, gk, eh, tri, pqk, *, seq):
    t = x2.shape[0]
    tm, kt = PROJ_TILE, K_TILE
    row = lambda i: (i, 0)
    outs = [
        jax.ShapeDtypeStruct((t // seq, seq // kt, N_HEADS * LANES, kt), jnp.bfloat16),
        jax.ShapeDtypeStruct((t, N_HEADS * LANES), jnp.bfloat16),
        jax.ShapeDtypeStruct((t // seq, seq // kt, N_HEADS * VT_ROWS, kt), jnp.bfloat16),
        jax.ShapeDtypeStruct((t, 2 * D_CONV + D_POOL), jnp.bfloat16),
    ]
    tile_t = lambda i: (i // (seq // tm), i % (seq // tm), 0, 0)
    return pl.pallas_call(
        functools.partial(_proj_kernel, tiles_per_seq=seq // tm),
        grid=(t // tm,),
        in_specs=[
            pl.BlockSpec((tm, D_MODEL), row),
            _layer_spec(layer, (1, D_MODEL)),
            _layer_spec(layer, (D_MODEL, 3 * D_ATTN)),
            _layer_spec(layer, (D_MODEL, LANES)),
            _layer_spec(layer, (D_MODEL, 3 * D_CONV + D_POOL)),
            _layer_spec(layer, (1, LANES)),
            _layer_spec(layer, (1, D_ATTN)),
            _layer_spec(layer, (1, D_ATTN)),
            _const_spec((D_ATTN, D_ATTN)),
            _const_spec((kt, kt)),
            _const_spec((N_PIECES * LANES, 2 * LANES)),
        ],
        out_specs=[
            pl.BlockSpec((1, tm // kt, N_HEADS * LANES, kt), tile_t),
            pl.BlockSpec((tm, N_HEADS * LANES), row),
            pl.BlockSpec((1, tm // kt, N_HEADS * VT_ROWS, kt), tile_t),
            pl.BlockSpec((tm, 2 * D_CONV + D_POOL), row),
        ],
        out_shape=outs,
        scratch_shapes=[pltpu.VMEM((1, LANES), jnp.float32)],
        compiler_params=pltpu.CompilerParams(dimension_semantics=("arbitrary",), vmem_limit_bytes=VMEM_LIMIT),
        name="proj",
    )(x2, g, wqkv, wf, wc, fb, gq, gk, eh, tri, pqk)


def _attn_kernel(qt_ref, kh_ref, vt_ref, o_ref, acc_sc, aux_sc, *, online):
    i = pl.program_id(1)
    tq, tk = Q_TILE, K_TILE
    heads = range(N_HEADS)

    acc_sc[...] = jnp.zeros_like(acc_sc)
    m_sc = p_sc = aux_sc
    if online:
        m_sc[...] = jnp.full_like(m_sc, NEG_BIG)

    def logits(kt, hh, masked, k0=0, k1=tk, q0=0):
        start = pl.multiple_of(kt * tk + k0, math.gcd(tk, k0))
        s = _dot(kh_ref[0, pl.ds(start, k1 - k0), hh * LANES:(hh + 1) * LANES],
                 qt_ref[0, 0, hh * LANES:(hh + 1) * LANES, q0:])
        if masked:
            k_id = lax.broadcasted_iota(jnp.int32, s.shape, 0) + (kt - i) * tk + k0
            q_id = lax.broadcasted_iota(jnp.int32, s.shape, 1) + q0
            s = jnp.where(k_id <= q_id, s, NEG_BIG)
        return s

    def vt_tile(kt, hh):
        return vt_ref[0, kt, hh * VT_ROWS:(hh + 1) * VT_ROWS, :]

    if online:
        def step(kt, masked):
            for hh in heads:
                s = logits(kt, hh, masked)
                m_prev = m_sc[hh]
                m_new = jnp.maximum(m_prev, jnp.max(s, axis=0, keepdims=True))
                m_sc[hh] = m_new
                acc_sc[hh] = jnp.exp2(m_prev - m_new) * acc_sc[hh] + _dot(vt_tile(kt, hh), _bf16(jnp.exp2(s - m_new)))

        def body(kt, carry):
            step(kt, False)
            return carry

        lax.fori_loop(0, i, body, 0)
        step(i, True)
    else:
        def stage(kt, rd, wr):
            for hh in heads:
                p_sc[wr, hh] = _bf16(jnp.exp2(logits(kt + 1, hh, False)))
                acc_sc[hh] += _dot(vt_tile(kt, hh), p_sc[rd, hh])

        def first_tile(masked):
            for hh in heads:
                p0 = _bf16(jnp.exp2(logits(0, hh, masked)))
                p_sc[0, hh] = p0
                p_sc[1, hh] = p0

        pl.when(i == 0)(functools.partial(first_tile, True))
        pl.when(i > 0)(functools.partial(first_tile, False))

        n_plain = jnp.maximum(i - 1, 0)

        @pl.when(n_plain % 2 == 1)
        def _():
            stage(0, 0, 1)

        def body(j, carry):
            kt = n_plain % 2 + 2 * j
            stage(kt, 1, 0)
            stage(kt + 1, 0, 1)
            return carry

        lax.fori_loop(0, n_plain // 2, body, 0)

        half = tk // 2

        @pl.when(i > 0)
        def _():
            for hh in heads:
                p_sc[0, hh, 0:half, :] = _bf16(jnp.exp2(logits(i, hh, True, 0, half)))
                p_sc[0, hh, half:tk, half:tq] = _bf16(jnp.exp2(logits(i, hh, True, half, tk, half)))
                acc_sc[hh] += _dot(vt_tile(i - 1, hh), p_sc[1, hh])

        for hh in heads:
            vt = vt_tile(i, hh)
            early = _dot(vt[:, 0:half], p_sc[0, hh, 0:half, :])
            late = _dot(vt[:, half:tk], p_sc[0, hh, half:tk, half:tq])
            acc_sc[hh, :, 0:half] += early[:, 0:half]
            acc_sc[hh, :, half:tq] += early[:, half:tq] + late

    for pr in range(N_HEADS // 2):
        o_t = []
        for hh in (2 * pr, 2 * pr + 1):
            acc = acc_sc[hh]
            o_t.append(acc[0:HEAD_DIM, :] / acc[HEAD_DIM:HEAD_DIM + 1, :])
        o_ref[0, :, pr * LANES:(pr + 1) * LANES] = _bf16(jnp.concatenate(o_t, axis=0).T)


def _attn_call(qt, kh, vt, *, online):
    b, s, _ = kh.shape
    tq = Q_TILE
    scratch = [pltpu.VMEM((N_HEADS, VT_ROWS, tq), jnp.float32),
               pltpu.VMEM((N_HEADS, 1, tq), jnp.float32) if online else pltpu.VMEM((2, N_HEADS, K_TILE, tq), jnp.bfloat16)]
    return pl.pallas_call(
        functools.partial(_attn_kernel, online=online),
        grid=(b, s // tq),
        in_specs=[
            pl.BlockSpec((1, 1, N_HEADS * LANES, tq), lambda bi, i: (bi, i, 0, 0)),
            pl.BlockSpec((1, s, N_HEADS * LANES), lambda bi, i: (bi, 0, 0)),
            pl.BlockSpec((1, s // K_TILE, N_HEADS * VT_ROWS, K_TILE), lambda bi, i: (bi, 0, 0, 0)),
        ],
        out_specs=pl.BlockSpec((1, tq, D_ATTN), lambda bi, i: (bi, i, 0)),
        out_shape=jax.ShapeDtypeStruct((b, s, D_ATTN), jnp.bfloat16),
        scratch_shapes=scratch,
        compiler_params=pltpu.CompilerParams(
            dimension_semantics=("arbitrary", "arbitrary"), vmem_limit_bytes=VMEM_LIMIT),
        name="attn_online" if online else "attn",
    )(qt, kh, vt)


def _shift_rows(x, k):
    return pltpu.roll(x, k, axis=0)


def _branch_inputs(cp_ref, halo_ref, cw_ref, seq_tile):
    tm = cp_ref.shape[0]
    halo = halo_ref[...].astype(jnp.float32) * jnp.where(seq_tile == 0, 0.0, 1.0)
    cp = cp_ref[...].astype(jnp.float32)
    z = jnp.concatenate([halo[:, 0:D_CONV], cp[:, 0:D_CONV]], axis=0)
    px = jnp.concatenate([halo[:, 2 * D_CONV:], cp[:, 2 * D_CONV:]], axis=0)
    cb = cp[:, D_CONV:2 * D_CONV]

    cw = cw_ref[...]
    conv = cw[2:3, :] * z + cw[1:2, :] * _shift_rows(z, 1) + cw[0:1, :] * _shift_rows(z, 2)
    u = _bf16(cb * conv[HALO:, :])

    s2 = px + _shift_rows(px, 1)
    s4 = s2 + _shift_rows(s2, 2)
    s8 = s4 + _shift_rows(s4, 4)
    s16 = s8 + _shift_rows(s8, 8)
    grp = lax.broadcasted_iota(jnp.int32, (tm, D_POOL), 1) // POOL_GROUP_DIM
    wsum = jnp.where(grp == 0, s2[HALO:], jnp.where(grp == 1, s4[HALO:], jnp.where(grp == 2, s8[HALO:], s16[HALO:])))
    pos = seq_tile * tm + lax.broadcasted_iota(jnp.int32, (tm, D_POOL), 0)
    win = jnp.left_shift(2, grp)
    counts = jnp.minimum(pos + 1, win).astype(jnp.float32)
    d = _bf16(wsum / counts - px[HALO:])
    return u, d


def _merge_rows(x, a, u, d, g_ref, wg_ref, wao_ref, wco_ref, wpool_ref, ps_ref, wo_ref):
    h = _bf16(x * _rms_scale(x) * g_ref[...])
    y_attn = _dot(a, wao_ref[...])
    y_conv = _dot(u, wco_ref[...])
    y_pool = _dot(d, wpool_ref[...]) * ps_ref[...]
    merged = jax.nn.sigmoid(_dot(h, wg_ref[:, 0:D_MODEL])) * y_attn
    merged += jax.nn.sigmoid(_dot(h, wg_ref[:, D_MODEL:2 * D_MODEL])) * y_conv
    merged += jax.nn.sigmoid(_dot(h, wg_ref[:, 2 * D_MODEL:3 * D_MODEL])) * y_pool
    return x + _dot(_bf16(merged), wo_ref[...])


def _ffn_rows(x, g_ref, wi_ref, wo_ref):
    h = _bf16(x * _rms_scale(x) * g_ref[...])
    acc = x
    for c0, c1 in FF_CHUNKS:
        gt = _dot(h, wi_ref[:, c0:c1])
        up = _dot(h, wi_ref[:, D_FF + c0:D_FF + c1])
        act = _bf16(gt * jax.nn.sigmoid(gt) * up)
        acc = acc + _dot(act, wo_ref[c0:c1, :])
    return acc


def _mix_ffn_kernel(x_ref, a_ref, cp_ref, halo_ref, g_ref, wg_ref, wao_ref, cw_ref, wco_ref, wpool_ref, ps_ref, wo_ref,
                    g2_ref, wi_ref, wfo_ref, o_ref, *, tiles_per_seq):
    seq_tile = pl.program_id(0) % tiles_per_seq
    u, d = _branch_inputs(cp_ref, halo_ref, cw_ref, seq_tile)
    for r0 in range(0, x_ref.shape[0], SUB_ROWS):
        rows = slice(r0, r0 + SUB_ROWS)
        x1 = _merge_rows(x_ref[rows, :], a_ref[rows, :], u[rows], d[rows], g_ref, wg_ref, wao_ref, wco_ref, wpool_ref,
                         ps_ref, wo_ref)
        o_ref[rows, :] = _ffn_rows(x1, g2_ref, wi_ref, wfo_ref)


def _mix_ffn_call(layer, x2, a2, cp, g, wg, wao, cw, wco, wpool, ps, wo, g2, wi, wfo, *, seq):
    t = x2.shape[0]
    tm = ROW_TILE
    row = lambda i: (i, 0)
    cpw = 2 * D_CONV + D_POOL
    halo_blocks = tm // HALO
    return pl.pallas_call(
        functools.partial(_mix_ffn_kernel, tiles_per_seq=seq // tm),
        grid=(t // tm,),
        in_specs=[
            pl.BlockSpec((tm, D_MODEL), row),
            pl.BlockSpec((tm, D_ATTN), row),
            pl.BlockSpec((tm, cpw), row),
            pl.BlockSpec((HALO, cpw), lambda i: (jnp.maximum(i * halo_blocks - 1, 0), 0)),
            _layer_spec(layer, (1, D_MODEL)),
            _layer_spec(layer, (D_MODEL, 3 * D_MODEL)),
            _layer_spec(layer, (D_ATTN, D_MODEL)),
            _layer_spec(layer, (8, D_CONV)),
            _layer_spec(layer, (D_CONV, D_MODEL)),
            _layer_spec(layer, (D_POOL, D_MODEL)),
            _layer_spec(layer, (1, D_MODEL)),
            _layer_spec(layer, (D_MODEL, D_MODEL)),
            _layer_spec(layer, (1, D_MODEL)),
            _layer_spec(layer, (D_MODEL, 2 * D_FF)),
            _layer_spec(layer, (D_FF, D_MODEL)),
        ],
        out_specs=pl.BlockSpec((tm, D_MODEL), row),
        out_shape=jax.ShapeDtypeStruct((t, D_MODEL), jnp.float32),
        compiler_params=pltpu.CompilerParams(dimension_semantics=("arbitrary",), vmem_limit_bytes=VMEM_LIMIT),
        name="mix_ffn",
    )(x2, a2, cp, cp, g, wg, wao, cw, wco, wpool, ps, wo, g2, wi, wfo)


def _placement_matrices():
    pq = np.zeros((N_PIECES * LANES, LANES), np.float32)
    pk = np.zeros((N_PIECES * LANES, LANES), np.float32)
    ones_row = N_HEADS
    for h in range(N_HEADS):
        for j in range(N_PIECES):
            pq[j * LANES + h, N_PIECES * h + j] = 1.0
            pk[ones_row, N_PIECES * h + j] = 1.0
            pq[ones_row, X_ONES + N_PIECES * h + j] = 1.0
            pk[j * LANES + h, X_ONES + N_PIECES * h + j] = -1.0
    return jnp.asarray(np.concatenate([pq, pk], axis=1), jnp.bfloat16)


def kernel(x, norm_mix_g, w_in, forget_b, q_norm_g, k_norm_g, w_attn_out, conv_w, w_conv_out, pool_w, pool_scale,
           w_o, norm_ffn_g, w_ffn_in, w_ffn_out):
    bsz, seq, _ = x.shape
    depth = w_in.shape[0]
    assert seq % ROW_TILE == 0 and seq % PROJ_TILE == 0 and PROJ_TILE % K_TILE == 0 and Q_TILE == K_TILE

    bf = jnp.bfloat16
    eh = jnp.asarray(np.kron(np.eye(N_HEADS), np.ones((HEAD_DIM, HEAD_DIM))), bf)
    tri = jnp.asarray(np.tril(np.ones((K_TILE, K_TILE))), bf)
    pqk = _placement_matrices()

    row3 = lambda p: p[:, None, :]
    w_qkv = w_in[:, :, :IN_F].astype(bf)
    w_f = jnp.pad(w_in[:, :, IN_F:IN_C], ((0, 0), (0, 0), (0, LANES - N_HEADS))).astype(bf)
    w_c = w_in[:, :, IN_C:IN_G].astype(bf)
    w_g = w_in[:, :, IN_G:].astype(bf)
    fb = row3(jnp.pad(forget_b, ((0, 0), (0, LANES - N_HEADS))))
    q_scale = (HEAD_DIM ** -0.5) * LOG2E
    gq = row3(jnp.tile(q_norm_g, (1, N_HEADS))) * q_scale
    gk = row3(jnp.tile(k_norm_g, (1, N_HEADS)))
    g_mix, g_ffn, p_scale = row3(norm_mix_g), row3(norm_ffn_g), row3(pool_scale)
    wao = w_attn_out.astype(bf)
    cw = jnp.pad(conv_w, ((0, 0), (0, 8 - CONV_K), (0, 0)))
    wco = w_conv_out.astype(bf)
    grp_mask = jnp.asarray(np.kron(np.eye(len(POOL_WINDOWS)), np.ones((POOL_GROUP_DIM, POOL_OUT_DIM))), jnp.float32)
    wpool = (jnp.tile(pool_w.reshape(depth, D_POOL, POOL_OUT_DIM), (1, 1, len(POOL_WINDOWS))) * grp_mask).astype(bf)
    wo = w_o.astype(bf)
    wfi = w_ffn_in.astype(bf)
    wfo = w_ffn_out.astype(bf)

    x2 = x.reshape(bsz * seq, D_MODEL)
    for l in range(depth):
        qt, kh, vt, cp = _proj_call(l, x2, g_mix, w_qkv, w_f, w_c, fb, gq, gk, eh, tri, pqk, seq=seq)
        logit_bound = HEAD_DIM * jnp.max(jnp.abs(gq[l])) * jnp.max(jnp.abs(gk[l]))
        a = lax.cond(logit_bound < UNSHIFTED_LOGIT_LIMIT,
                     functools.partial(_attn_call, online=False), functools.partial(_attn_call, online=True),
                     qt, kh.reshape(bsz, seq, N_HEADS * LANES), vt)
        x2 = _mix_ffn_call(l, x2, a.reshape(bsz * seq, D_ATTN), cp, g_mix, w_g, wao, cw, wco, wpool, p_scale, wo,
                           g_ffn, wfi, wfo, seq=seq)
    return x2.reshape(bsz, seq, D_MODEL)
```

```python
import functools
import math

import numpy as np
import jax
import jax.numpy as jnp
from jax import lax
from jax.experimental import pallas as pl
from jax.experimental.pallas import tpu as pltpu

D_MODEL = 1024
HEAD_DIM = 64
D_ATTN = D_MODEL // 2
N_HEADS = D_ATTN // HEAD_DIM
D_CONV = D_MODEL // 4
CONV_K = 3
D_POOL = D_MODEL // 4
POOL_WINDOWS = (2, 4, 8, 16)
POOL_GROUP_DIM = D_POOL // len(POOL_WINDOWS)
POOL_OUT_DIM = D_MODEL // len(POOL_WINDOWS)
D_FF = -(-8 * D_MODEL // (3 * 256)) * 256
EPS = 1e-6

LANES = 128
F32_SUBLANES = 8
BF16_SUBLANES = 16
VT_ROWS = HEAD_DIM + BF16_SUBLANES
HALO = BF16_SUBLANES
assert HALO >= max(POOL_WINDOWS) - 1 and HALO >= CONV_K - 1
N_PIECES = 3
LOG2E = 1.4426950408889634
NEG_BIG = -1e30
UNSHIFTED_LOGIT_LIMIT = 100.0

ROW_TILE = 512
PROJ_TILE = 1024
SUB_ROWS = 256
Q_TILE = 512
K_TILE = 512
FF_CHUNKS = ((0, 1024), (1024, 2048), (2048, D_FF))
VMEM_LIMIT = 56 * 1024 * 1024

IN_F = 3 * D_ATTN
IN_C = IN_F + N_HEADS
IN_G = IN_C + 3 * D_CONV + D_POOL
C_CX, C_CB, C_CC, C_PX = 0, D_CONV, 2 * D_CONV, 3 * D_CONV

X_ONES = N_HEADS * N_PIECES


def _bf16(x):
    return x.astype(jnp.bfloat16)


def _dot(a, b):
    return jnp.dot(a, b, preferred_element_type=jnp.float32)


def _split3(x):
    hi = _bf16(x)
    r1 = x - hi.astype(jnp.float32)
    mid = _bf16(r1)
    lo = _bf16(r1 - mid.astype(jnp.float32))
    return hi, mid, lo


def _rms_scale(x):
    return lax.rsqrt(jnp.mean(x * x, axis=-1, keepdims=True) + EPS)


def _proj_kernel(x_ref, g_ref, wqkv_ref, wf_ref, wc_ref, fb_ref, gq_ref, gk_ref, eh_ref, tri_ref, pqk_ref,
                 qt_ref, kh_ref, vt_ref, cp_ref, carry_ref, *, tiles_per_seq):
    i = pl.program_id(0)

    @pl.when(i % tiles_per_seq == 0)
    def _():
        carry_ref[...] = jnp.zeros_like(carry_ref)

    sub = K_TILE
    lane = lax.broadcasted_iota(jnp.int32, (sub, LANES), 1)
    ex_keep = ((lane >= HEAD_DIM) & (lane < HEAD_DIM + N_PIECES)) \
        | ((lane >= HEAD_DIM + X_ONES) & (lane < HEAD_DIM + X_ONES + N_PIECES))
    ones_rows = _bf16(jnp.where(lax.broadcasted_iota(jnp.int32, (VT_ROWS - HEAD_DIM, sub), 0) == 0, 1.0, 0.0))
    carry = carry_ref[...]

    for j in range(x_ref.shape[0] // sub):
        rows = slice(j * sub, (j + 1) * sub)
        x = x_ref[rows, :]
        h = _bf16(x * _rms_scale(x) * g_ref[...])

        f = _dot(h, wf_ref[...]) + fb_ref[...]
        logf = jnp.minimum(f, 0.0) - jnp.log1p(jnp.exp(-jnp.abs(f)))
        logf = jnp.where(lane < N_HEADS, logf, 0.0)
        pieces = jnp.concatenate(_split3(logf)[:2], axis=1)
        cum = _dot(tri_ref[...], pieces)
        c = cum[:, 0:LANES] + cum[:, LANES:2 * LANES] + carry
        carry = c[sub - 1:, :]

        ap = jnp.concatenate(_split3(jnp.where(lane == N_HEADS, 1.0, c * LOG2E)), axis=1)
        ex = _dot(ap, pqk_ref[...])
        exq, exk = ex[:, 0:LANES], ex[:, LANES:2 * LANES]

        q = _dot(h, wqkv_ref[:, 0:D_ATTN])
        k = _dot(h, wqkv_ref[:, D_ATTN:2 * D_ATTN])
        ssq = _dot(_bf16(q * q), eh_ref[...])
        ssk = _dot(_bf16(k * k), eh_ref[...])
        qn = q * lax.rsqrt(ssq * (1.0 / HEAD_DIM) + EPS) * gq_ref[...]
        kn = k * lax.rsqrt(ssk * (1.0 / HEAD_DIM) + EPS) * gk_ref[...]

        for hd in range(N_HEADS):
            sl = slice((hd // 2) * LANES, (hd // 2 + 1) * LANES)
            blk_q, blk_k = qn[:, sl], kn[:, sl]
            if hd % 2:
                blk_q = pltpu.roll(blk_q, HEAD_DIM, axis=1)
                blk_k = pltpu.roll(blk_k, HEAD_DIM, axis=1)
            shift = HEAD_DIM - N_PIECES * hd
            qh = jnp.where(lane < HEAD_DIM, blk_q, jnp.where(ex_keep, pltpu.roll(exq, shift, axis=1), 0.0))
            kh = jnp.where(lane < HEAD_DIM, blk_k, jnp.where(ex_keep, pltpu.roll(exk, shift, axis=1), 0.0))
            qt_ref[0, j, hd * LANES:(hd + 1) * LANES, :] = _bf16(qh.T)
            kh_ref[rows, hd * LANES:(hd + 1) * LANES] = _bf16(kh)

        v = _dot(h, wqkv_ref[:, 2 * D_ATTN:3 * D_ATTN])
        for pr in range(N_HEADS // 2):
            vt = _bf16(v[:, pr * LANES:(pr + 1) * LANES].T)
            for hh in range(2):
                base = (2 * pr + hh) * VT_ROWS
                vt_ref[0, j, base:base + HEAD_DIM, :] = vt[hh * HEAD_DIM:(hh + 1) * HEAD_DIM, :]
                vt_ref[0, j, base + HEAD_DIM:base + VT_ROWS, :] = ones_rows

        r = _dot(h, wc_ref[...])
        cp_ref[rows, 0:D_CONV] = _bf16(r[:, C_CC:C_CC + D_CONV] * r[:, C_CX:C_CX + D_CONV])
        cp_ref[rows, D_CONV:2 * D_CONV] = _bf16(r[:, C_CB:C_CB + D_CONV])
        cp_ref[rows, 2 * D_CONV:2 * D_CONV + D_POOL] = _bf16(r[:, C_PX:C_PX + D_POOL])

    carry_ref[...] = carry


def _const_spec(shape):
    return pl.BlockSpec(shape, lambda *_: (0,) * len(shape), pipeline_mode=pl.Buffered(1))


def _layer_spec(layer, shape):
    return pl.BlockSpec((None,) + shape, lambda *_: (layer,) + (0,) * len(shape), pipeline_mode=pl.Buffered(1))


def _proj_call(layer, x2, g, wqkv, wf, wc, fb, gq, gk, eh, tri, pqk, *, seq):
    t = x2.shape[0]
    tm, kt = PROJ_TILE, K_TILE
    row = lambda i: (i, 0)
    outs = [
        jax.ShapeDtypeStruct((t // seq, seq // kt, N_HEADS * LANES, kt), jnp.bfloat16),
        jax.ShapeDtypeStruct((t, N_HEADS * LANES), jnp.bfloat16),
        jax.ShapeDtypeStruct((t // seq, seq // kt, N_HEADS * VT_ROWS, kt), jnp.bfloat16),
        jax.ShapeDtypeStruct((t, 2 * D_CONV + D_POOL), jnp.bfloat16),
    ]
    tile_t = lambda i: (i // (seq // tm), i % (seq // tm), 0, 0)
    return pl.pallas_call(
        functools.partial(_proj_kernel, tiles_per_seq=seq // tm),
        grid=(t // tm,),
        in_specs=[
            pl.BlockSpec((tm, D_MODEL), row),
            _layer_spec(layer, (1, D_MODEL)),
            _layer_spec(layer, (D_MODEL, 3 * D_ATTN)),
            _layer_spec(layer, (D_MODEL, LANES)),
            _layer_spec(layer, (D_MODEL, 3 * D_CONV + D_POOL)),
            _layer_spec(layer, (1, LANES)),
            _layer_spec(layer, (1, D_ATTN)),
            _layer_spec(layer, (1, D_ATTN)),
            _const_spec((D_ATTN, D_ATTN)),
            _const_spec((kt, kt)),
            _const_spec((N_PIECES * LANES, 2 * LANES)),
        ],
        out_specs=[
            pl.BlockSpec((1, tm // kt, N_HEADS * LANES, kt), tile_t),
            pl.BlockSpec((tm, N_HEADS * LANES), row),
            pl.BlockSpec((1, tm // kt, N_HEADS * VT_ROWS, kt), tile_t),
            pl.BlockSpec((tm, 2 * D_CONV + D_POOL), row),
        ],
        out_shape=outs,
        scratch_shapes=[pltpu.VMEM((1, LANES), jnp.float32)],
        compiler_params=pltpu.CompilerParams(dimension_semantics=("arbitrary",), vmem_limit_bytes=VMEM_LIMIT),
        name="proj",
    )(x2, g, wqkv, wf, wc, fb, gq, gk, eh, tri, pqk)


def _attn_kernel(qt_ref, kh_ref, vt_ref, o_ref, acc_sc, aux_sc, *, online):
    i = pl.program_id(1)
    tq, tk = Q_TILE, K_TILE
    heads = range(N_HEADS)

    acc_sc[...] = jnp.zeros_like(acc_sc)
    m_sc = p_sc = aux_sc
    if online:
        m_sc[...] = jnp.full_like(m_sc, NEG_BIG)

    def logits(kt, hh, masked, k0=0, k1=tk, q0=0):
        start = pl.multiple_of(kt * tk + k0, math.gcd(tk, k0))
        s = _dot(kh_ref[0, pl.ds(start, k1 - k0), hh * LANES:(hh + 1) * LANES],
                 qt_ref[0, 0, hh * LANES:(hh + 1) * LANES, q0:])
        if masked:
            k_id = lax.broadcasted_iota(jnp.int32, s.shape, 0) + (kt - i) * tk + k0
            q_id = lax.broadcasted_iota(jnp.int32, s.shape, 1) + q0
            s = jnp.where(k_id <= q_id, s, NEG_BIG)
        return s

    def vt_tile(kt, hh):
        return vt_ref[0, kt, hh * VT_ROWS:(hh + 1) * VT_ROWS, :]

    if online:
        def step(kt, masked):
            for hh in heads:
                s = logits(kt, hh, masked)
                m_prev = m_sc[hh]
                m_new = jnp.maximum(m_prev, jnp.max(s, axis=0, keepdims=True))
                m_sc[hh] = m_new
                acc_sc[hh] = jnp.exp2(m_prev - m_new) * acc_sc[hh] + _dot(vt_tile(kt, hh), _bf16(jnp.exp2(s - m_new)))

        def body(kt, carry):
            step(kt, False)
            return carry

        lax.fori_loop(0, i, body, 0)
        step(i, True)
    else:
        def stage(kt, rd, wr):
            for hh in heads:
                p_sc[wr, hh] = _bf16(jnp.exp2(logits(kt + 1, hh, False)))
                acc_sc[hh] += _dot(vt_tile(kt, hh), p_sc[rd, hh])

        def first_tile(masked):
            for hh in heads:
                p0 = _bf16(jnp.exp2(logits(0, hh, masked)))
                p_sc[0, hh] = p0
                p_sc[1, hh] = p0

        pl.when(i == 0)(functools.partial(first_tile, True))
        pl.when(i > 0)(functools.partial(first_tile, False))

        n_plain = jnp.maximum(i - 1, 0)

        @pl.when(n_plain % 2 == 1)
        def _():
            stage(0, 0, 1)

        def body(j, carry):
            kt = n_plain % 2 + 2 * j
            stage(kt, 1, 0)
            stage(kt + 1, 0, 1)
            return carry

        lax.fori_loop(0, n_plain // 2, body, 0)

        half = tk // 2

        @pl.when(i > 0)
        def _():
            for hh in heads:
                p_sc[0, hh, 0:half, :] = _bf16(jnp.exp2(logits(i, hh, True, 0, half)))
                p_sc[0, hh, half:tk, half:tq] = _bf16(jnp.exp2(logits(i, hh, True, half, tk, half)))
                acc_sc[hh] += _dot(vt_tile(i - 1, hh), p_sc[1, hh])

        for hh in heads:
            vt = vt_tile(i, hh)
            early = _dot(vt[:, 0:half], p_sc[0, hh, 0:half, :])
            late = _dot(vt[:, half:tk], p_sc[0, hh, half:tk, half:tq])
            acc_sc[hh, :, 0:half] += early[:, 0:half]
            acc_sc[hh, :, half:tq] += early[:, half:tq] + late

    for pr in range(N_HEADS // 2):
        o_t = []
        for hh in (2 * pr, 2 * pr + 1):
            acc = acc_sc[hh]
            o_t.append(acc[0:HEAD_DIM, :] / acc[HEAD_DIM:HEAD_DIM + 1, :])
        o_ref[0, :, pr * LANES:(pr + 1) * LANES] = _bf16(jnp.concatenate(o_t, axis=0).T)


def _attn_call(qt, kh, vt, *, online):
    b, s, _ = kh.shape
    tq = Q_TILE
    scratch = [pltpu.VMEM((N_HEADS, VT_ROWS, tq), jnp.float32),
               pltpu.VMEM((N_HEADS, 1, tq), jnp.float32) if online else pltpu.VMEM((2, N_HEADS, K_TILE, tq), jnp.bfloat16)]
    return pl.pallas_call(
        functools.partial(_attn_kernel, online=online),
        grid=(b, s // tq),
        in_specs=[
            pl.BlockSpec((1, 1, N_HEADS * LANES, tq), lambda bi, i: (bi, i, 0, 0)),
            pl.BlockSpec((1, s, N_HEADS * LANES), lambda bi, i: (bi, 0, 0)),
            pl.BlockSpec((1, s // K_TILE, N_HEADS * VT_ROWS, K_TILE), lambda bi, i: (bi, 0, 0, 0)),
        ],
        out_specs=pl.BlockSpec((1, tq, D_ATTN), lambda bi, i: (bi, i, 0)),
        out_shape=jax.ShapeDtypeStruct((b, s, D_ATTN), jnp.bfloat16),
        scratch_shapes=scratch,
        compiler_params=pltpu.CompilerParams(
            dimension_semantics=("arbitrary", "arbitrary"), vmem_limit_bytes=VMEM_LIMIT),
        name="attn_online" if online else "attn",
    )(qt, kh, vt)


def _shift_rows(x, k):
    return pltpu.roll(x, k, axis=0)


def _branch_inputs(cp_ref, halo_ref, cw_ref, seq_tile):
    tm = cp_ref.shape[0]
    halo = halo_ref[...].astype(jnp.float32) * jnp.where(seq_tile == 0, 0.0, 1.0)
    cp = cp_ref[...].astype(jnp.float32)
    z = jnp.concatenate([halo[:, 0:D_CONV], cp[:, 0:D_CONV]], axis=0)
    px = jnp.concatenate([halo[:, 2 * D_CONV:], cp[:, 2 * D_CONV:]], axis=0)
    cb = cp[:, D_CONV:2 * D_CONV]

    cw = cw_ref[...]
    conv = cw[2:3, :] * z + cw[1:2, :] * _shift_rows(z, 1) + cw[0:1, :] * _shift_rows(z, 2)
    u = _bf16(cb * conv[HALO:, :])

    s2 = px + _shift_rows(px, 1)
    s4 = s2 + _shift_rows(s2, 2)
    s8 = s4 + _shift_rows(s4, 4)
    s16 = s8 + _shift_rows(s8, 8)
    grp = lax.broadcasted_iota(jnp.int32, (tm, D_POOL), 1) // POOL_GROUP_DIM
    wsum = jnp.where(grp == 0, s2[HALO:], jnp.where(grp == 1, s4[HALO:], jnp.where(grp == 2, s8[HALO:], s16[HALO:])))
    pos = seq_tile * tm + lax.broadcasted_iota(jnp.int32, (tm, D_POOL), 0)
    win = jnp.left_shift(2, grp)
    counts = jnp.minimum(pos + 1, win).astype(jnp.float32)
    d = _bf16(wsum / counts - px[HALO:])
    return u, d


def _merge_rows(x, a, u, d, g_ref, wg_ref, wao_ref, wco_ref, wpool_ref, ps_ref, wo_ref):
    h = _bf16(x * _rms_scale(x) * g_ref[...])
    y_attn = _dot(a, wao_ref[...])
    y_conv = _dot(u, wco_ref[...])
    y_pool = _dot(d, wpool_ref[...]) * ps_ref[...]
    merged = jax.nn.sigmoid(_dot(h, wg_ref[:, 0:D_MODEL])) * y_attn
    merged += jax.nn.sigmoid(_dot(h, wg_ref[:, D_MODEL:2 * D_MODEL])) * y_conv
    merged += jax.nn.sigmoid(_dot(h, wg_ref[:, 2 * D_MODEL:3 * D_MODEL])) * y_pool
    return x + _dot(_bf16(merged), wo_ref[...])


def _ffn_rows(x, g_ref, wi_ref, wo_ref):
    h = _bf16(x * _rms_scale(x) * g_ref[...])
    acc = x
    for c0, c1 in FF_CHUNKS:
        gt = _dot(h, wi_ref[:, c0:c1])
        up = _dot(h, wi_ref[:, D_FF + c0:D_FF + c1])
        act = _bf16(gt * jax.nn.sigmoid(gt) * up)
        acc = acc + _dot(act, wo_ref[c0:c1, :])
    return acc


def _mix_ffn_kernel(x_ref, a_ref, cp_ref, halo_ref, g_ref, wg_ref, wao_ref, cw_ref, wco_ref, wpool_ref, ps_ref, wo_ref,
                    g2_ref, wi_ref, wfo_ref, o_ref, *, tiles_per_seq):
    seq_tile = pl.program_id(0) % tiles_per_seq
    u, d = _branch_inputs(cp_ref, halo_ref, cw_ref, seq_tile)
    for r0 in range(0, x_ref.shape[0], SUB_ROWS):
        rows = slice(r0, r0 + SUB_ROWS)
        x1 = _merge_rows(x_ref[rows, :], a_ref[rows, :], u[rows], d[rows], g_ref, wg_ref, wao_ref, wco_ref, wpool_ref,
                         ps_ref, wo_ref)
        o_ref[rows, :] = _ffn_rows(x1, g2_ref, wi_ref, wfo_ref)


def _mix_ffn_call(layer, x2, a2, cp, g, wg, wao, cw, wco, wpool, ps, wo, g2, wi, wfo, *, seq):
    t = x2.shape[0]
    tm = ROW_TILE
    row = lambda i: (i, 0)
    cpw = 2 * D_CONV + D_POOL
    halo_blocks = tm // HALO
    return pl.pallas_call(
        functools.partial(_mix_ffn_kernel, tiles_per_seq=seq // tm),
        grid=(t // tm,),
        in_specs=[
            pl.BlockSpec((tm, D_MODEL), row),
            pl.BlockSpec((tm, D_ATTN), row),
            pl.BlockSpec((tm, cpw), row),
            pl.BlockSpec((HALO, cpw), lambda i: (jnp.maximum(i * halo_blocks - 1, 0), 0)),
            _layer_spec(layer, (1, D_MODEL)),
            _layer_spec(layer, (D_MODEL, 3 * D_MODEL)),
            _layer_spec(layer, (D_ATTN, D_MODEL)),
            _layer_spec(layer, (F32_SUBLANES, D_CONV)),
            _layer_spec(layer, (D_CONV, D_MODEL)),
            _layer_spec(layer, (D_POOL, D_MODEL)),
            _layer_spec(layer, (1, D_MODEL)),
            _layer_spec(layer, (D_MODEL, D_MODEL)),
            _layer_spec(layer, (1, D_MODEL)),
            _layer_spec(layer, (D_MODEL, 2 * D_FF)),
            _layer_spec(layer, (D_FF, D_MODEL)),
        ],
        out_specs=pl.BlockSpec((tm, D_MODEL), row),
        out_shape=jax.ShapeDtypeStruct((t, D_MODEL), jnp.float32),
        compiler_params=pltpu.CompilerParams(dimension_semantics=("arbitrary",), vmem_limit_bytes=VMEM_LIMIT),
        name="mix_ffn",
    )(x2, a2, cp, cp, g, wg, wao, cw, wco, wpool, ps, wo, g2, wi, wfo)


def _placement_matrices():
    pq = np.zeros((N_PIECES * LANES, LANES), np.float32)
    pk = np.zeros((N_PIECES * LANES, LANES), np.float32)
    ones_row = N_HEADS
    for h in range(N_HEADS):
        for j in range(N_PIECES):
            pq[j * LANES + h, N_PIECES * h + j] = 1.0
            pk[ones_row, N_PIECES * h + j] = 1.0
            pq[ones_row, X_ONES + N_PIECES * h + j] = 1.0
            pk[j * LANES + h, X_ONES + N_PIECES * h + j] = -1.0
    return jnp.asarray(np.concatenate([pq, pk], axis=1), jnp.bfloat16)


def kernel(x, norm_mix_g, w_in, forget_b, q_norm_g, k_norm_g, w_attn_out, conv_w, w_conv_out, pool_w, pool_scale,
           w_o, norm_ffn_g, w_ffn_in, w_ffn_out):
    bsz, seq, _ = x.shape
    depth = w_in.shape[0]
    assert seq % ROW_TILE == 0 and seq % PROJ_TILE == 0 and PROJ_TILE % K_TILE == 0 and Q_TILE == K_TILE

    bf = jnp.bfloat16
    eh = jnp.asarray(np.kron(np.eye(N_HEADS), np.ones((HEAD_DIM, HEAD_DIM))), bf)
    tri = jnp.asarray(np.tril(np.ones((K_TILE, K_TILE))), bf)
    pqk = _placement_matrices()

    row3 = lambda p: p[:, None, :]
    w_qkv = w_in[:, :, :IN_F].astype(bf)
    w_f = jnp.pad(w_in[:, :, IN_F:IN_C], ((0, 0), (0, 0), (0, LANES - N_HEADS))).astype(bf)
    w_c = w_in[:, :, IN_C:IN_G].astype(bf)
    w_g = w_in[:, :, IN_G:].astype(bf)
    fb = row3(jnp.pad(forget_b, ((0, 0), (0, LANES - N_HEADS))))
    q_scale = (HEAD_DIM ** -0.5) * LOG2E
    gq = row3(jnp.tile(q_norm_g, (1, N_HEADS))) * q_scale
    gk = row3(jnp.tile(k_norm_g, (1, N_HEADS)))
    g_mix, g_ffn, p_scale = row3(norm_mix_g), row3(norm_ffn_g), row3(pool_scale)
    wao = w_attn_out.astype(bf)
    cw = jnp.pad(conv_w, ((0, 0), (0, F32_SUBLANES - CONV_K), (0, 0)))
    wco = w_conv_out.astype(bf)
    grp_mask = jnp.asarray(np.kron(np.eye(len(POOL_WINDOWS)), np.ones((POOL_GROUP_DIM, POOL_OUT_DIM))), jnp.float32)
    wpool = (jnp.tile(pool_w.reshape(depth, D_POOL, POOL_OUT_DIM), (1, 1, len(POOL_WINDOWS))) * grp_mask).astype(bf)
    wo = w_o.astype(bf)
    wfi = w_ffn_in.astype(bf)
    wfo = w_ffn_out.astype(bf)

    x2 = x.reshape(bsz * seq, D_MODEL)
    for l in range(depth):
        qt, kh, vt, cp = _proj_call(l, x2, g_mix, w_qkv, w_f, w_c, fb, gq, gk, eh, tri, pqk, seq=seq)
        logit_bound = HEAD_DIM * jnp.max(jnp.abs(gq[l])) * jnp.max(jnp.abs(gk[l]))
        a = lax.cond(logit_bound < UNSHIFTED_LOGIT_LIMIT,
                     functools.partial(_attn_call, online=False), functools.partial(_attn_call, online=True),
                     qt, kh.reshape(bsz, seq, N_HEADS * LANES), vt)
        x2 = _mix_ffn_call(l, x2, a.reshape(bsz * seq, D_ATTN), cp, g_mix, w_g, wao, cw, wco, wpool, p_scale, wo,
                           g_ffn, wfi, wfo, seq=seq)
    return x2.reshape(bsz, seq, D_MODEL)
```

```python
import functools
import math

import numpy as np
import jax
import jax.numpy as jnp
from jax import lax
from jax.experimental import pallas as pl
from jax.experimental.pallas import tpu as pltpu

D_MODEL = 1024
HEAD_DIM = 64
D_ATTN = D_MODEL // 2
N_HEADS = D_ATTN // HEAD_DIM
D_CONV = D_MODEL // 4
CONV_K = 3
D_POOL = D_MODEL // 4
POOL_WINDOWS = (2, 4, 8, 16)
POOL_GROUP_DIM = D_POOL // len(POOL_WINDOWS)
POOL_OUT_DIM = D_MODEL // len(POOL_WINDOWS)
D_FF = -(-8 * D_MODEL // (3 * 256)) * 256
EPS = 1e-6

LANES = 128
F32_SUBLANES = 8
BF16_SUBLANES = 16
VT_ROWS = HEAD_DIM + BF16_SUBLANES
HALO = BF16_SUBLANES
assert HALO >= max(POOL_WINDOWS) - 1 and HALO >= CONV_K - 1
N_PIECES = 3
LOG2E = 1.4426950408889634
NEG_BIG = -1e30
UNSHIFTED_LOGIT_LIMIT = 100.0

ROW_TILE = 512
PROJ_TILE = 1024
SUB_ROWS = 256
Q_TILE = 512
K_TILE = 512
FF_CHUNKS = ((0, 1024), (1024, 2048), (2048, D_FF))
VMEM_LIMIT = 56 * 1024 * 1024

IN_F = 3 * D_ATTN
IN_C = IN_F + N_HEADS
IN_G = IN_C + 3 * D_CONV + D_POOL
C_CX, C_CB, C_CC, C_PX = 0, D_CONV, 2 * D_CONV, 3 * D_CONV

X_ONES = N_HEADS * N_PIECES


def _bf16(x):
    return x.astype(jnp.bfloat16)


def _dot(a, b):
    return jnp.dot(a, b, preferred_element_type=jnp.float32)


def _split3(x):
    hi = _bf16(x)
    r1 = x - hi.astype(jnp.float32)
    mid = _bf16(r1)
    lo = _bf16(r1 - mid.astype(jnp.float32))
    return hi, mid, lo


def _rms_scale(x):
    return lax.rsqrt(jnp.mean(x * x, axis=-1, keepdims=True) + EPS)


def _proj_kernel(x_ref, g_ref, wqkv_ref, wf_ref, wc_ref, fb_ref, gq_ref, gk_ref, eh_ref, tri_ref, pqk_ref,
                 qt_ref, kh_ref, vt_ref, cp_ref, carry_ref, *, tiles_per_seq):
    i = pl.program_id(0)

    @pl.when(i % tiles_per_seq == 0)
    def _():
        carry_ref[...] = jnp.zeros_like(carry_ref)

    sub = K_TILE
    lane = lax.broadcasted_iota(jnp.int32, (sub, LANES), 1)
    ex_keep = ((lane >= HEAD_DIM) & (lane < HEAD_DIM + N_PIECES)) \
        | ((lane >= HEAD_DIM + X_ONES) & (lane < HEAD_DIM + X_ONES + N_PIECES))
    ones_rows = _bf16(jnp.where(lax.broadcasted_iota(jnp.int32, (VT_ROWS - HEAD_DIM, sub), 0) == 0, 1.0, 0.0))
    carry = carry_ref[...]

    for j in range(x_ref.shape[0] // sub):
        rows = slice(j * sub, (j + 1) * sub)
        x = x_ref[rows, :]
        h = _bf16(x * _rms_scale(x) * g_ref[...])

        f = _dot(h, wf_ref[...]) + fb_ref[...]
        logf = jnp.minimum(f, 0.0) - jnp.log1p(jnp.exp(-jnp.abs(f)))
        logf = jnp.where(lane < N_HEADS, logf, 0.0)
        pieces = jnp.concatenate(_split3(logf)[:2], axis=1)
        cum = _dot(tri_ref[...], pieces)
        c = cum[:, 0:LANES] + cum[:, LANES:2 * LANES] + carry
        carry = c[sub - 1:, :]

        ap = jnp.concatenate(_split3(jnp.where(lane == N_HEADS, 1.0, c * LOG2E)), axis=1)
        ex = _dot(ap, pqk_ref[...])
        exq, exk = ex[:, 0:LANES], ex[:, LANES:2 * LANES]

        q = _dot(h, wqkv_ref[:, 0:D_ATTN])
        k = _dot(h, wqkv_ref[:, D_ATTN:2 * D_ATTN])
        ssq = _dot(_bf16(q * q), eh_ref[...])
        ssk = _dot(_bf16(k * k), eh_ref[...])
        qn = q * lax.rsqrt(ssq * (1.0 / HEAD_DIM) + EPS) * gq_ref[...]
        kn = k * lax.rsqrt(ssk * (1.0 / HEAD_DIM) + EPS) * gk_ref[...]

        for hd in range(N_HEADS):
            sl = slice((hd // 2) * LANES, (hd // 2 + 1) * LANES)
            blk_q, blk_k = qn[:, sl], kn[:, sl]
            if hd % 2:
                blk_q = pltpu.roll(blk_q, HEAD_DIM, axis=1)
                blk_k = pltpu.roll(blk_k, HEAD_DIM, axis=1)
            shift = HEAD_DIM - N_PIECES * hd
            qh = jnp.where(lane < HEAD_DIM, blk_q, jnp.where(ex_keep, pltpu.roll(exq, shift, axis=1), 0.0))
            kh = jnp.where(lane < HEAD_DIM, blk_k, jnp.where(ex_keep, pltpu.roll(exk, shift, axis=1), 0.0))
            qt_ref[0, j, hd * LANES:(hd + 1) * LANES, :] = _bf16(qh.T)
            kh_ref[rows, hd * LANES:(hd + 1) * LANES] = _bf16(kh)

        v = _dot(h, wqkv_ref[:, 2 * D_ATTN:3 * D_ATTN])
        for pr in range(N_HEADS // 2):
            vt = _bf16(v[:, pr * LANES:(pr + 1) * LANES].T)
            for hh in range(2):
                base = (2 * pr + hh) * VT_ROWS
                vt_ref[0, j, base:base + HEAD_DIM, :] = vt[hh * HEAD_DIM:(hh + 1) * HEAD_DIM, :]
                vt_ref[0, j, base + HEAD_DIM:base + VT_ROWS, :] = ones_rows

        r = _dot(h, wc_ref[...])
        cp_ref[rows, 0:D_CONV] = _bf16(r[:, C_CC:C_CC + D_CONV] * r[:, C_CX:C_CX + D_CONV])
        cp_ref[rows, D_CONV:2 * D_CONV] = _bf16(r[:, C_CB:C_CB + D_CONV])
        cp_ref[rows, 2 * D_CONV:2 * D_CONV + D_POOL] = _bf16(r[:, C_PX:C_PX + D_POOL])

    carry_ref[...] = carry


def _const_spec(shape):
    return pl.BlockSpec(shape, lambda *_: (0,) * len(shape), pipeline_mode=pl.Buffered(1))


def _layer_spec(layer, shape):
    return pl.BlockSpec((None,) + shape, lambda *_: (layer,) + (0,) * len(shape), pipeline_mode=pl.Buffered(1))


def _proj_call(layer, x2, g, wqkv, wf, wc, fb, gq, gk, eh, tri, pqk, *, seq):
    t = x2.shape[0]
    tm, kt = PROJ_TILE, K_TILE
    row = lambda i: (i, 0)
    outs = [
        jax.ShapeDtypeStruct((t // seq, seq // kt, N_HEADS * LANES, kt), jnp.bfloat16),
        jax.ShapeDtypeStruct((t, N_HEADS * LANES), jnp.bfloat16),
        jax.ShapeDtypeStruct((t // seq, seq // kt, N_HEADS * VT_ROWS, kt), jnp.bfloat16),
        jax.ShapeDtypeStruct((t, 2 * D_CONV + D_POOL), jnp.bfloat16),
    ]
    tile_t = lambda i: (i // (seq // tm), i % (seq // tm), 0, 0)
    return pl.pallas_call(
        functools.partial(_proj_kernel, tiles_per_seq=seq // tm),
        grid=(t // tm,),
        in_specs=[
            pl.BlockSpec((tm, D_MODEL), row),
            _layer_spec(layer, (1, D_MODEL)),
            _layer_spec(layer, (D_MODEL, 3 * D_ATTN)),
            _layer_spec(layer, (D_MODEL, LANES)),
            _layer_spec(layer, (D_MODEL, 3 * D_CONV + D_POOL)),
            _layer_spec(layer, (1, LANES)),
            _layer_spec(layer, (1, D_ATTN)),
            _layer_spec(layer, (1, D_ATTN)),
            _const_spec((D_ATTN, D_ATTN)),
            _const_spec((kt, kt)),
            _const_spec((N_PIECES * LANES, 2 * LANES)),
        ],
        out_specs=[
            pl.BlockSpec((1, tm // kt, N_HEADS * LANES, kt), tile_t),
            pl.BlockSpec((tm, N_HEADS * LANES), row),
            pl.BlockSpec((1, tm // kt, N_HEADS * VT_ROWS, kt), tile_t),
            pl.BlockSpec((tm, 2 * D_CONV + D_POOL), row),
        ],
        out_shape=outs,
        scratch_shapes=[pltpu.VMEM((1, LANES), jnp.float32)],
        compiler_params=pltpu.CompilerParams(dimension_semantics=("arbitrary",), vmem_limit_bytes=VMEM_LIMIT),
        name="proj",
    )(x2, g, wqkv, wf, wc, fb, gq, gk, eh, tri, pqk)


def _attn_kernel(qt_ref, kh_ref, vt_ref, o_ref, acc_sc, aux_sc, *, online):
    i = pl.program_id(1)
    tq, tk = Q_TILE, K_TILE
    heads = range(N_HEADS)

    m_sc = p_sc = aux_sc
    if online:
        acc_sc[...] = jnp.zeros_like(acc_sc)
        m_sc[...] = jnp.full_like(m_sc, NEG_BIG)

    def logits(kt, hh, masked, k0=0, k1=tk, q0=0):
        start = pl.multiple_of(kt * tk + k0, math.gcd(tk, k0))
        s = _dot(kh_ref[0, pl.ds(start, k1 - k0), hh * LANES:(hh + 1) * LANES],
                 qt_ref[0, 0, hh * LANES:(hh + 1) * LANES, q0:])
        if masked:
            k_id = lax.broadcasted_iota(jnp.int32, s.shape, 0) + (kt - i) * tk + k0
            q_id = lax.broadcasted_iota(jnp.int32, s.shape, 1) + q0
            s = jnp.where(k_id <= q_id, s, NEG_BIG)
        return s

    def vt_tile(kt, hh):
        return vt_ref[0, kt, hh * VT_ROWS:(hh + 1) * VT_ROWS, :]

    def write_output():
        for pr in range(N_HEADS // 2):
            o_t = []
            for hh in (2 * pr, 2 * pr + 1):
                acc = acc_sc[hh]
                o_t.append(acc[0:HEAD_DIM, :] / acc[HEAD_DIM:HEAD_DIM + 1, :])
            o_ref[0, :, pr * LANES:(pr + 1) * LANES] = _bf16(jnp.concatenate(o_t, axis=0).T)

    if online:
        def step(kt, masked):
            for hh in heads:
                s = logits(kt, hh, masked)
                m_prev = m_sc[hh]
                m_new = jnp.maximum(m_prev, jnp.max(s, axis=0, keepdims=True))
                m_sc[hh] = m_new
                acc_sc[hh] = jnp.exp2(m_prev - m_new) * acc_sc[hh] + _dot(vt_tile(kt, hh), _bf16(jnp.exp2(s - m_new)))

        def body(kt, carry):
            step(kt, False)
            return carry

        lax.fori_loop(0, i, body, 0)
        step(i, True)
        write_output()
    else:
        n_q = pl.num_programs(1) - 1
        half = tk // 2

        def stage(kt, rd, wr):
            for hh in heads:
                p_sc[wr, hh] = _bf16(jnp.exp2(logits(kt + 1, hh, False)))
                acc_sc[hh] += _dot(vt_tile(kt, hh), p_sc[rd, hh])

        def first_tile(masked, slots):
            for hh in heads:
                p0 = _bf16(jnp.exp2(logits(0, hh, masked)))
                for slot in slots:
                    p_sc[slot, hh] = p0

        def finish(prev):
            for hh in heads:
                vt = vt_tile(prev, hh)
                early = _dot(vt[:, 0:half], p_sc[2, hh, 0:half, :])
                late = _dot(vt[:, half:tk], p_sc[2, hh, half:tk, half:tq])
                acc_sc[hh, :, 0:half] += early[:, 0:half]
                acc_sc[hh, :, half:tq] += early[:, half:tq] + late
            write_output()
            acc_sc[...] = jnp.zeros_like(acc_sc)

        @pl.when(i == 0)
        def _():
            acc_sc[...] = jnp.zeros_like(acc_sc)
            first_tile(True, (2,))

        @pl.when((i > 0) & (i < n_q))
        def _():
            finish(i - 1)
            first_tile(False, (0, 1))

        @pl.when(i == n_q)
        def _():
            finish(i - 1)

        n_plain = jnp.where(i < n_q, jnp.maximum(i - 1, 0), 0)

        @pl.when(n_plain % 2 == 1)
        def _():
            stage(0, 0, 1)

        def body(j, carry):
            kt = n_plain % 2 + 2 * j
            stage(kt, 1, 0)
            stage(kt + 1, 0, 1)
            return carry

        lax.fori_loop(0, n_plain // 2, body, 0)

        @pl.when((i > 0) & (i < n_q))
        def _():
            for hh in heads:
                p_sc[2, hh, 0:half, :] = _bf16(jnp.exp2(logits(i, hh, True, 0, half)))
                p_sc[2, hh, half:tk, half:tq] = _bf16(jnp.exp2(logits(i, hh, True, half, tk, half)))
                acc_sc[hh] += _dot(vt_tile(i - 1, hh), p_sc[1, hh])


def _attn_call(qt, kh, vt, *, online):
    b, s, _ = kh.shape
    tq = Q_TILE
    n_q = s // tq
    scratch = [pltpu.VMEM((N_HEADS, VT_ROWS, tq), jnp.float32),
               pltpu.VMEM((N_HEADS, 1, tq), jnp.float32) if online else pltpu.VMEM((3, N_HEADS, K_TILE, tq), jnp.bfloat16)]
    if online:
        steps, q_idx, o_idx = n_q, (lambda i: i), (lambda i: i)
    else:
        steps, q_idx, o_idx = n_q + 1, (lambda i: jnp.minimum(i, n_q - 1)), (lambda i: jnp.maximum(i - 1, 0))
    return pl.pallas_call(
        functools.partial(_attn_kernel, online=online),
        grid=(b, steps),
        in_specs=[
            pl.BlockSpec((1, 1, N_HEADS * LANES, tq), lambda bi, i: (bi, q_idx(i), 0, 0)),
            pl.BlockSpec((1, s, N_HEADS * LANES), lambda bi, i: (bi, 0, 0)),
            pl.BlockSpec((1, s // K_TILE, N_HEADS * VT_ROWS, K_TILE), lambda bi, i: (bi, 0, 0, 0)),
        ],
        out_specs=pl.BlockSpec((1, tq, D_ATTN), lambda bi, i: (bi, o_idx(i), 0)),
        out_shape=jax.ShapeDtypeStruct((b, s, D_ATTN), jnp.bfloat16),
        scratch_shapes=scratch,
        compiler_params=pltpu.CompilerParams(
            dimension_semantics=("arbitrary", "arbitrary"), vmem_limit_bytes=VMEM_LIMIT),
        name="attn_online" if online else "attn",
    )(qt, kh, vt)


def _shift_rows(x, k):
    return pltpu.roll(x, k, axis=0)


def _branch_inputs(cp_ref, halo_ref, cw_ref, seq_tile):
    tm = cp_ref.shape[0]
    halo = halo_ref[...].astype(jnp.float32) * jnp.where(seq_tile == 0, 0.0, 1.0)
    cp = cp_ref[...].astype(jnp.float32)
    z = jnp.concatenate([halo[:, 0:D_CONV], cp[:, 0:D_CONV]], axis=0)
    px = jnp.concatenate([halo[:, 2 * D_CONV:], cp[:, 2 * D_CONV:]], axis=0)
    cb = cp[:, D_CONV:2 * D_CONV]

    cw = cw_ref[...]
    conv = cw[2:3, :] * z + cw[1:2, :] * _shift_rows(z, 1) + cw[0:1, :] * _shift_rows(z, 2)
    u = _bf16(cb * conv[HALO:, :])

    s2 = px + _shift_rows(px, 1)
    s4 = s2 + _shift_rows(s2, 2)
    s8 = s4 + _shift_rows(s4, 4)
    s16 = s8 + _shift_rows(s8, 8)
    grp = lax.broadcasted_iota(jnp.int32, (tm, D_POOL), 1) // POOL_GROUP_DIM
    wsum = jnp.where(grp == 0, s2[HALO:], jnp.where(grp == 1, s4[HALO:], jnp.where(grp == 2, s8[HALO:], s16[HALO:])))
    pos = seq_tile * tm + lax.broadcasted_iota(jnp.int32, (tm, D_POOL), 0)
    win = jnp.left_shift(2, grp)
    counts = jnp.minimum(pos + 1, win).astype(jnp.float32)
    d = _bf16(wsum / counts - px[HALO:])
    return u, d


def _merge_rows(x, a, u, d, g_ref, wg_ref, wao_ref, wco_ref, wpool_ref, ps_ref, wo_ref):
    h = _bf16(x * _rms_scale(x) * g_ref[...])
    y_attn = _dot(a, wao_ref[...])
    y_conv = _dot(u, wco_ref[...])
    y_pool = _dot(d, wpool_ref[...]) * ps_ref[...]
    merged = jax.nn.sigmoid(_dot(h, wg_ref[:, 0:D_MODEL])) * y_attn
    merged += jax.nn.sigmoid(_dot(h, wg_ref[:, D_MODEL:2 * D_MODEL])) * y_conv
    merged += jax.nn.sigmoid(_dot(h, wg_ref[:, 2 * D_MODEL:3 * D_MODEL])) * y_pool
    return x + _dot(_bf16(merged), wo_ref[...])


def _ffn_rows(x, g_ref, wi_ref, wo_ref):
    h = _bf16(x * _rms_scale(x) * g_ref[...])
    acc = x
    for c0, c1 in FF_CHUNKS:
        gt = _dot(h, wi_ref[:, c0:c1])
        up = _dot(h, wi_ref[:, D_FF + c0:D_FF + c1])
        act = _bf16(gt * jax.nn.sigmoid(gt) * up)
        acc = acc + _dot(act, wo_ref[c0:c1, :])
    return acc


def _mix_ffn_kernel(x_ref, a_ref, cp_ref, halo_ref, g_ref, wg_ref, wao_ref, cw_ref, wco_ref, wpool_ref, ps_ref, wo_ref,
                    g2_ref, wi_ref, wfo_ref, o_ref, *, tiles_per_seq):
    seq_tile = pl.program_id(0) % tiles_per_seq
    u, d = _branch_inputs(cp_ref, halo_ref, cw_ref, seq_tile)
    for r0 in range(0, x_ref.shape[0], SUB_ROWS):
        rows = slice(r0, r0 + SUB_ROWS)
        x1 = _merge_rows(x_ref[rows, :], a_ref[rows, :], u[rows], d[rows], g_ref, wg_ref, wao_ref, wco_ref, wpool_ref,
                         ps_ref, wo_ref)
        o_ref[rows, :] = _ffn_rows(x1, g2_ref, wi_ref, wfo_ref)


def _mix_ffn_call(layer, x2, a2, cp, g, wg, wao, cw, wco, wpool, ps, wo, g2, wi, wfo, *, seq):
    t = x2.shape[0]
    tm = ROW_TILE
    row = lambda i: (i, 0)
    cpw = 2 * D_CONV + D_POOL
    halo_blocks = tm // HALO
    return pl.pallas_call(
        functools.partial(_mix_ffn_kernel, tiles_per_seq=seq // tm),
        grid=(t // tm,),
        in_specs=[
            pl.BlockSpec((tm, D_MODEL), row),
            pl.BlockSpec((tm, D_ATTN), row),
            pl.BlockSpec((tm, cpw), row),
            pl.BlockSpec((HALO, cpw), lambda i: (jnp.maximum(i * halo_blocks - 1, 0), 0)),
            _layer_spec(layer, (1, D_MODEL)),
            _layer_spec(layer, (D_MODEL, 3 * D_MODEL)),
            _layer_spec(layer, (D_ATTN, D_MODEL)),
            _layer_spec(layer, (F32_SUBLANES, D_CONV)),
            _layer_spec(layer, (D_CONV, D_MODEL)),
            _layer_spec(layer, (D_POOL, D_MODEL)),
            _layer_spec(layer, (1, D_MODEL)),
            _layer_spec(layer, (D_MODEL, D_MODEL)),
            _layer_spec(layer, (1, D_MODEL)),
            _layer_spec(layer, (D_MODEL, 2 * D_FF)),
            _layer_spec(layer, (D_FF, D_MODEL)),
        ],
        out_specs=pl.BlockSpec((tm, D_MODEL), row),
        out_shape=jax.ShapeDtypeStruct((t, D_MODEL), jnp.float32),
        compiler_params=pltpu.CompilerParams(dimension_semantics=("arbitrary",), vmem_limit_bytes=VMEM_LIMIT),
        name="mix_ffn",
    )(x2, a2, cp, cp, g, wg, wao, cw, wco, wpool, ps, wo, g2, wi, wfo)


def _placement_matrices():
    pq = np.zeros((N_PIECES * LANES, LANES), np.float32)
    pk = np.zeros((N_PIECES * LANES, LANES), np.float32)
    ones_row = N_HEADS
    for h in range(N_HEADS):
        for j in range(N_PIECES):
            pq[j * LANES + h, N_PIECES * h + j] = 1.0
            pk[ones_row, N_PIECES * h + j] = 1.0
            pq[ones_row, X_ONES + N_PIECES * h + j] = 1.0
            pk[j * LANES + h, X_ONES + N_PIECES * h + j] = -1.0
    return jnp.asarray(np.concatenate([pq, pk], axis=1), jnp.bfloat16)


def kernel(x, norm_mix_g, w_in, forget_b, q_norm_g, k_norm_g, w_attn_out, conv_w, w_conv_out, pool_w, pool_scale,
           w_o, norm_ffn_g, w_ffn_in, w_ffn_out):
    bsz, seq, _ = x.shape
    depth = w_in.shape[0]
    assert seq % ROW_TILE == 0 and seq % PROJ_TILE == 0 and PROJ_TILE % K_TILE == 0 and Q_TILE == K_TILE

    bf = jnp.bfloat16
    eh = jnp.asarray(np.kron(np.eye(N_HEADS), np.ones((HEAD_DIM, HEAD_DIM))), bf)
    tri = jnp.asarray(np.tril(np.ones((K_TILE, K_TILE))), bf)
    pqk = _placement_matrices()

    row3 = lambda p: p[:, None, :]
    w_qkv = w_in[:, :, :IN_F].astype(bf)
    w_f = jnp.pad(w_in[:, :, IN_F:IN_C], ((0, 0), (0, 0), (0, LANES - N_HEADS))).astype(bf)
    w_c = w_in[:, :, IN_C:IN_G].astype(bf)
    w_g = w_in[:, :, IN_G:].astype(bf)
    fb = row3(jnp.pad(forget_b, ((0, 0), (0, LANES - N_HEADS))))
    q_scale = (HEAD_DIM ** -0.5) * LOG2E
    gq = row3(jnp.tile(q_norm_g, (1, N_HEADS))) * q_scale
    gk = row3(jnp.tile(k_norm_g, (1, N_HEADS)))
    g_mix, g_ffn, p_scale = row3(norm_mix_g), row3(norm_ffn_g), row3(pool_scale)
    wao = w_attn_out.astype(bf)
    cw = jnp.pad(conv_w, ((0, 0), (0, F32_SUBLANES - CONV_K), (0, 0)))
    wco = w_conv_out.astype(bf)
    grp_mask = jnp.asarray(np.kron(np.eye(len(POOL_WINDOWS)), np.ones((POOL_GROUP_DIM, POOL_OUT_DIM))), jnp.float32)
    wpool = (jnp.tile(pool_w.reshape(depth, D_POOL, POOL_OUT_DIM), (1, 1, len(POOL_WINDOWS))) * grp_mask).astype(bf)
    wo = w_o.astype(bf)
    wfi = w_ffn_in.astype(bf)
    wfo = w_ffn_out.astype(bf)

    x2 = x.reshape(bsz * seq, D_MODEL)
    for l in range(depth):
        qt, kh, vt, cp = _proj_call(l, x2, g_mix, w_qkv, w_f, w_c, fb, gq, gk, eh, tri, pqk, seq=seq)
        logit_bound = HEAD_DIM * jnp.max(jnp.abs(gq[l])) * jnp.max(jnp.abs(gk[l]))
        a = lax.cond(logit_bound < UNSHIFTED_LOGIT_LIMIT,
                     functools.partial(_attn_call, online=False), functools.partial(_attn_call, online=True),
                     qt, kh.reshape(bsz, seq, N_HEADS * LANES), vt)
        x2 = _mix_ffn_call(l, x2, a.reshape(bsz * seq, D_ATTN), cp, g_mix, w_g, wao, cw, wco, wpool, p_scale, wo,
                           g_ffn, wfi, wfo, seq=seq)
    return x2.reshape(bsz, seq, D_MODEL)
```

```python
import functools
import math

import numpy as np
import jax
import jax.numpy as jnp
from jax import lax
from jax.experimental import pallas as pl
from jax.experimental.pallas import tpu as pltpu

D_MODEL = 1024
HEAD_DIM = 64
D_ATTN = D_MODEL // 2
N_HEADS = D_ATTN // HEAD_DIM
D_CONV = D_MODEL // 4
CONV_K = 3
D_POOL = D_MODEL // 4
POOL_WINDOWS = (2, 4, 8, 16)
POOL_GROUP_DIM = D_POOL // len(POOL_WINDOWS)
POOL_OUT_DIM = D_MODEL // len(POOL_WINDOWS)
D_FF = -(-8 * D_MODEL // (3 * 256)) * 256
EPS = 1e-6

LANES = 128
F32_SUBLANES = 8
BF16_SUBLANES = 16
VT_ROWS = HEAD_DIM + BF16_SUBLANES
HALO = BF16_SUBLANES
assert HALO >= max(POOL_WINDOWS) - 1 and HALO >= CONV_K - 1
N_PIECES = 3
LOG2E = 1.4426950408889634
NEG_BIG = -1e30
UNSHIFTED_LOGIT_LIMIT = 100.0

ROW_TILE = 512
PROJ_TILE = 1024
SUB_ROWS = 256
Q_TILE = 512
K_TILE = 512
FF_CHUNKS = ((0, 1024), (1024, 2048), (2048, D_FF))
VMEM_LIMIT = 56 * 1024 * 1024

IN_F = 3 * D_ATTN
IN_C = IN_F + N_HEADS
IN_G = IN_C + 3 * D_CONV + D_POOL
C_CX, C_CB, C_CC, C_PX = 0, D_CONV, 2 * D_CONV, 3 * D_CONV

X_ONES = N_HEADS * N_PIECES


def _bf16(x):
    return x.astype(jnp.bfloat16)


def _dot(a, b):
    return jnp.dot(a, b, preferred_element_type=jnp.float32)


def _split3(x):
    hi = _bf16(x)
    r1 = x - hi.astype(jnp.float32)
    mid = _bf16(r1)
    lo = _bf16(r1 - mid.astype(jnp.float32))
    return hi, mid, lo


def _rms_scale(x):
    return lax.rsqrt(jnp.mean(x * x, axis=-1, keepdims=True) + EPS)


def _proj_kernel(x_ref, g_ref, wqkv_ref, wf_ref, wc_ref, fb_ref, gq_ref, gk_ref, eh_ref, tri_ref, pqk_ref,
                 qt_ref, kh_ref, vt_ref, cp_ref, carry_ref, *, tiles_per_seq):
    i = pl.program_id(0)

    @pl.when(i % tiles_per_seq == 0)
    def _():
        carry_ref[...] = jnp.zeros_like(carry_ref)

    sub = K_TILE
    lane = lax.broadcasted_iota(jnp.int32, (sub, LANES), 1)
    ex_keep = ((lane >= HEAD_DIM) & (lane < HEAD_DIM + N_PIECES)) \
        | ((lane >= HEAD_DIM + X_ONES) & (lane < HEAD_DIM + X_ONES + N_PIECES))
    ones_rows = _bf16(jnp.where(lax.broadcasted_iota(jnp.int32, (VT_ROWS - HEAD_DIM, sub), 0) == 0, 1.0, 0.0))
    carry = carry_ref[...]

    for j in range(x_ref.shape[0] // sub):
        rows = slice(j * sub, (j + 1) * sub)
        x = x_ref[rows, :]
        h = _bf16(x * _rms_scale(x) * g_ref[...])

        f = _dot(h, wf_ref[...]) + fb_ref[...]
        logf = jnp.minimum(f, 0.0) - jnp.log1p(jnp.exp(-jnp.abs(f)))
        logf = jnp.where(lane < N_HEADS, logf, 0.0)
        pieces = jnp.concatenate(_split3(logf)[:2], axis=1)
        cum = _dot(tri_ref[...], pieces)
        c = cum[:, 0:LANES] + cum[:, LANES:2 * LANES] + carry
        carry = c[sub - 1:, :]

        ap = jnp.concatenate(_split3(jnp.where(lane == N_HEADS, 1.0, c * LOG2E)), axis=1)
        ex = _dot(ap, pqk_ref[...])
        exq, exk = ex[:, 0:LANES], ex[:, LANES:2 * LANES]

        q = _dot(h, wqkv_ref[:, 0:D_ATTN])
        k = _dot(h, wqkv_ref[:, D_ATTN:2 * D_ATTN])
        ssq = _dot(_bf16(q * q), eh_ref[...])
        ssk = _dot(_bf16(k * k), eh_ref[...])
        qn = q * lax.rsqrt(ssq * (1.0 / HEAD_DIM) + EPS) * gq_ref[...]
        kn = k * lax.rsqrt(ssk * (1.0 / HEAD_DIM) + EPS) * gk_ref[...]

        for hd in range(N_HEADS):
            sl = slice((hd // 2) * LANES, (hd // 2 + 1) * LANES)
            blk_q, blk_k = qn[:, sl], kn[:, sl]
            if hd % 2:
                blk_q = pltpu.roll(blk_q, HEAD_DIM, axis=1)
                blk_k = pltpu.roll(blk_k, HEAD_DIM, axis=1)
            shift = HEAD_DIM - N_PIECES * hd
            qh = jnp.where(lane < HEAD_DIM, blk_q, jnp.where(ex_keep, pltpu.roll(exq, shift, axis=1), 0.0))
            kh = jnp.where(lane < HEAD_DIM, blk_k, jnp.where(ex_keep, pltpu.roll(exk, shift, axis=1), 0.0))
            qt_ref[0, j, hd * LANES:(hd + 1) * LANES, :] = _bf16(qh.T)
            kh_ref[rows, hd * LANES:(hd + 1) * LANES] = _bf16(kh)

        v = _dot(h, wqkv_ref[:, 2 * D_ATTN:3 * D_ATTN])
        for pr in range(N_HEADS // 2):
            vt = _bf16(v[:, pr * LANES:(pr + 1) * LANES].T)
            for hh in range(2):
                base = (2 * pr + hh) * VT_ROWS
                vt_ref[0, j, base:base + HEAD_DIM, :] = vt[hh * HEAD_DIM:(hh + 1) * HEAD_DIM, :]
                vt_ref[0, j, base + HEAD_DIM:base + VT_ROWS, :] = ones_rows

        r = _dot(h, wc_ref[...])
        cp_ref[rows, 0:D_CONV] = _bf16(r[:, C_CC:C_CC + D_CONV] * r[:, C_CX:C_CX + D_CONV])
        cp_ref[rows, D_CONV:2 * D_CONV] = _bf16(r[:, C_CB:C_CB + D_CONV])
        cp_ref[rows, 2 * D_CONV:2 * D_CONV + D_POOL] = _bf16(r[:, C_PX:C_PX + D_POOL])

    carry_ref[...] = carry


def _const_spec(shape):
    return pl.BlockSpec(shape, lambda *_: (0,) * len(shape), pipeline_mode=pl.Buffered(1))


def _layer_spec(layer, shape):
    return pl.BlockSpec((None,) + shape, lambda *_: (layer,) + (0,) * len(shape), pipeline_mode=pl.Buffered(1))


def _proj_call(layer, x2, g, wqkv, wf, wc, fb, gq, gk, eh, tri, pqk, *, seq):
    t = x2.shape[0]
    tm, kt = PROJ_TILE, K_TILE
    row = lambda i: (i, 0)
    outs = [
        jax.ShapeDtypeStruct((t // seq, seq // kt, N_HEADS * LANES, kt), jnp.bfloat16),
        jax.ShapeDtypeStruct((t, N_HEADS * LANES), jnp.bfloat16),
        jax.ShapeDtypeStruct((t // seq, seq // kt, N_HEADS * VT_ROWS, kt), jnp.bfloat16),
        jax.ShapeDtypeStruct((t, 2 * D_CONV + D_POOL), jnp.bfloat16),
    ]
    tile_t = lambda i: (i // (seq // tm), i % (seq // tm), 0, 0)
    return pl.pallas_call(
        functools.partial(_proj_kernel, tiles_per_seq=seq // tm),
        grid=(t // tm,),
        in_specs=[
            pl.BlockSpec((tm, D_MODEL), row),
            _layer_spec(layer, (1, D_MODEL)),
            _layer_spec(layer, (D_MODEL, 3 * D_ATTN)),
            _layer_spec(layer, (D_MODEL, LANES)),
            _layer_spec(layer, (D_MODEL, 3 * D_CONV + D_POOL)),
            _layer_spec(layer, (1, LANES)),
            _layer_spec(layer, (1, D_ATTN)),
            _layer_spec(layer, (1, D_ATTN)),
            _const_spec((D_ATTN, D_ATTN)),
            _const_spec((kt, kt)),
            _const_spec((N_PIECES * LANES, 2 * LANES)),
        ],
        out_specs=[
            pl.BlockSpec((1, tm // kt, N_HEADS * LANES, kt), tile_t),
            pl.BlockSpec((tm, N_HEADS * LANES), row),
            pl.BlockSpec((1, tm // kt, N_HEADS * VT_ROWS, kt), tile_t),
            pl.BlockSpec((tm, 2 * D_CONV + D_POOL), row),
        ],
        out_shape=outs,
        scratch_shapes=[pltpu.VMEM((1, LANES), jnp.float32)],
        compiler_params=pltpu.CompilerParams(dimension_semantics=("arbitrary",), vmem_limit_bytes=VMEM_LIMIT),
        name="proj",
    )(x2, g, wqkv, wf, wc, fb, gq, gk, eh, tri, pqk)


def _attn_kernel(qt_ref, kh_ref, vt_ref, o_ref, acc_sc, aux_sc, *, online):
    i = pl.program_id(1)
    tq, tk = Q_TILE, K_TILE
    heads = range(N_HEADS)

    acc_sc[...] = jnp.zeros_like(acc_sc)
    m_sc = p_sc = aux_sc
    if online:
        m_sc[...] = jnp.full_like(m_sc, NEG_BIG)

    def logits(kt, hh, masked, k0=0, k1=tk, q0=0):
        start = pl.multiple_of(kt * tk + k0, math.gcd(tk, k0))
        s = _dot(kh_ref[0, pl.ds(start, k1 - k0), hh * LANES:(hh + 1) * LANES],
                 qt_ref[0, 0, hh * LANES:(hh + 1) * LANES, q0:])
        if masked:
            k_id = lax.broadcasted_iota(jnp.int32, s.shape, 0) + (kt - i) * tk + k0
            q_id = lax.broadcasted_iota(jnp.int32, s.shape, 1) + q0
            s = jnp.where(k_id <= q_id, s, NEG_BIG)
        return s

    def vt_tile(kt, hh):
        return vt_ref[0, kt, hh * VT_ROWS:(hh + 1) * VT_ROWS, :]

    if online:
        def step(kt, masked):
            for hh in heads:
                s = logits(kt, hh, masked)
                m_prev = m_sc[hh]
                m_new = jnp.maximum(m_prev, jnp.max(s, axis=0, keepdims=True))
                m_sc[hh] = m_new
                acc_sc[hh] = jnp.exp2(m_prev - m_new) * acc_sc[hh] + _dot(vt_tile(kt, hh), _bf16(jnp.exp2(s - m_new)))

        def body(kt, carry):
            step(kt, False)
            return carry

        lax.fori_loop(0, i, body, 0)
        step(i, True)
    else:
        def stage(kt, rd, wr):
            for hh in heads:
                p_sc[wr, hh] = _bf16(jnp.exp2(logits(kt + 1, hh, False)))
                acc_sc[hh] += _dot(vt_tile(kt, hh), p_sc[rd, hh])

        def first_tile(masked):
            for hh in heads:
                p0 = _bf16(jnp.exp2(logits(0, hh, masked)))
                p_sc[0, hh] = p0
                p_sc[1, hh] = p0

        pl.when(i == 0)(functools.partial(first_tile, True))
        pl.when(i > 0)(functools.partial(first_tile, False))

        n_plain = jnp.maximum(i - 1, 0)

        @pl.when(n_plain % 2 == 1)
        def _():
            stage(0, 0, 1)

        def body(j, carry):
            kt = n_plain % 2 + 2 * j
            stage(kt, 1, 0)
            stage(kt + 1, 0, 1)
            return carry

        lax.fori_loop(0, n_plain // 2, body, 0)

        half = tk // 2

        @pl.when(i > 0)
        def _():
            for hh in heads:
                p_sc[0, hh, 0:half, :] = _bf16(jnp.exp2(logits(i, hh, True, 0, half)))
                p_sc[0, hh, half:tk, half:tq] = _bf16(jnp.exp2(logits(i, hh, True, half, tk, half)))
                acc_sc[hh] += _dot(vt_tile(i - 1, hh), p_sc[1, hh])

        for hh in heads:
            vt = vt_tile(i, hh)
            early = _dot(vt[:, 0:half], p_sc[0, hh, 0:half, :])
            late = _dot(vt[:, half:tk], p_sc[0, hh, half:tk, half:tq])
            acc_sc[hh, :, 0:half] += early[:, 0:half]
            acc_sc[hh, :, half:tq] += early[:, half:tq] + late

    for pr in range(N_HEADS // 2):
        o_t = []
        for hh in (2 * pr, 2 * pr + 1):
            acc = acc_sc[hh]
            o_t.append(acc[0:HEAD_DIM, :] / acc[HEAD_DIM:HEAD_DIM + 1, :])
        o_ref[0, :, pr * LANES:(pr + 1) * LANES] = _bf16(jnp.concatenate(o_t, axis=0).T)


def _attn_call(qt, kh, vt, *, online):
    b, s, _ = kh.shape
    tq = Q_TILE
    scratch = [pltpu.VMEM((N_HEADS, VT_ROWS, tq), jnp.float32),
               pltpu.VMEM((N_HEADS, 1, tq), jnp.float32) if online else pltpu.VMEM((2, N_HEADS, K_TILE, tq), jnp.bfloat16)]
    return pl.pallas_call(
        functools.partial(_attn_kernel, online=online),
        grid=(b, s // tq),
        in_specs=[
            pl.BlockSpec((1, 1, N_HEADS * LANES, tq), lambda bi, i: (bi, i, 0, 0)),
            pl.BlockSpec((1, s, N_HEADS * LANES), lambda bi, i: (bi, 0, 0)),
            pl.BlockSpec((1, s // K_TILE, N_HEADS * VT_ROWS, K_TILE), lambda bi, i: (bi, 0, 0, 0)),
        ],
        out_specs=pl.BlockSpec((1, tq, D_ATTN), lambda bi, i: (bi, i, 0)),
        out_shape=jax.ShapeDtypeStruct((b, s, D_ATTN), jnp.bfloat16),
        scratch_shapes=scratch,
        compiler_params=pltpu.CompilerParams(
            dimension_semantics=("arbitrary", "arbitrary"), vmem_limit_bytes=VMEM_LIMIT),
        name="attn_online" if online else "attn",
    )(qt, kh, vt)


def _shift_rows(x, k):
    return pltpu.roll(x, k, axis=0)


def _branch_inputs(cp_ref, halo_ref, cw_ref, seq_tile):
    tm = cp_ref.shape[0]
    halo = halo_ref[...].astype(jnp.float32) * jnp.where(seq_tile == 0, 0.0, 1.0)
    cp = cp_ref[...].astype(jnp.float32)
    z = jnp.concatenate([halo[:, 0:D_CONV], cp[:, 0:D_CONV]], axis=0)
    px = jnp.concatenate([halo[:, 2 * D_CONV:], cp[:, 2 * D_CONV:]], axis=0)
    cb = cp[:, D_CONV:2 * D_CONV]

    cw = cw_ref[...]
    conv = cw[2:3, :] * z + cw[1:2, :] * _shift_rows(z, 1) + cw[0:1, :] * _shift_rows(z, 2)
    u = _bf16(cb * conv[HALO:, :])

    s2 = px + _shift_rows(px, 1)
    s4 = s2 + _shift_rows(s2, 2)
    s8 = s4 + _shift_rows(s4, 4)
    s16 = s8 + _shift_rows(s8, 8)
    grp = lax.broadcasted_iota(jnp.int32, (tm, D_POOL), 1) // POOL_GROUP_DIM
    wsum = jnp.where(grp == 0, s2[HALO:], jnp.where(grp == 1, s4[HALO:], jnp.where(grp == 2, s8[HALO:], s16[HALO:])))
    pos = seq_tile * tm + lax.broadcasted_iota(jnp.int32, (tm, D_POOL), 0)
    win = jnp.left_shift(2, grp)
    counts = jnp.minimum(pos + 1, win).astype(jnp.float32)
    d = _bf16(wsum / counts - px[HALO:])
    return u, d


def _merge_rows(x, a, u, d, g_ref, wg_ref, wao_ref, wco_ref, wpool_ref, ps_ref, wo_ref):
    h = _bf16(x * _rms_scale(x) * g_ref[...])
    y_attn = _dot(a, wao_ref[...])
    y_conv = _dot(u, wco_ref[...])
    y_pool = _dot(d, wpool_ref[...]) * ps_ref[...]
    merged = jax.nn.sigmoid(_dot(h, wg_ref[:, 0:D_MODEL])) * y_attn
    merged += jax.nn.sigmoid(_dot(h, wg_ref[:, D_MODEL:2 * D_MODEL])) * y_conv
    merged += jax.nn.sigmoid(_dot(h, wg_ref[:, 2 * D_MODEL:3 * D_MODEL])) * y_pool
    return x + _dot(_bf16(merged), wo_ref[...])


def _ffn_rows(x, g_ref, wi_ref, wo_ref):
    h = _bf16(x * _rms_scale(x) * g_ref[...])
    acc = x
    for c0, c1 in FF_CHUNKS:
        gt = _dot(h, wi_ref[:, c0:c1])
        up = _dot(h, wi_ref[:, D_FF + c0:D_FF + c1])
        act = _bf16(gt * jax.nn.sigmoid(gt) * up)
        acc = acc + _dot(act, wo_ref[c0:c1, :])
    return acc


def _mix_ffn_kernel(x_ref, a_ref, cp_ref, halo_ref, g_ref, wg_ref, wao_ref, cw_ref, wco_ref, wpool_ref, ps_ref, wo_ref,
                    g2_ref, wi_ref, wfo_ref, o_ref, *, tiles_per_seq):
    seq_tile = pl.program_id(0) % tiles_per_seq
    u, d = _branch_inputs(cp_ref, halo_ref, cw_ref, seq_tile)
    blocks = [slice(r0, r0 + SUB_ROWS) for r0 in range(0, x_ref.shape[0], SUB_ROWS)]
    x1 = [_merge_rows(x_ref[rows, :], a_ref[rows, :], u[rows], d[rows], g_ref, wg_ref, wao_ref, wco_ref, wpool_ref,
                      ps_ref, wo_ref) for rows in blocks]
    for rows, x1_rows in zip(blocks, x1):
        o_ref[rows, :] = _ffn_rows(x1_rows, g2_ref, wi_ref, wfo_ref)


def _mix_ffn_call(layer, x2, a2, cp, g, wg, wao, cw, wco, wpool, ps, wo, g2, wi, wfo, *, seq):
    t = x2.shape[0]
    tm = ROW_TILE
    row = lambda i: (i, 0)
    cpw = 2 * D_CONV + D_POOL
    halo_blocks = tm // HALO
    return pl.pallas_call(
        functools.partial(_mix_ffn_kernel, tiles_per_seq=seq // tm),
        grid=(t // tm,),
        in_specs=[
            pl.BlockSpec((tm, D_MODEL), row),
            pl.BlockSpec((tm, D_ATTN), row),
            pl.BlockSpec((tm, cpw), row),
            pl.BlockSpec((HALO, cpw), lambda i: (jnp.maximum(i * halo_blocks - 1, 0), 0)),
            _layer_spec(layer, (1, D_MODEL)),
            _layer_spec(layer, (D_MODEL, 3 * D_MODEL)),
            _layer_spec(layer, (D_ATTN, D_MODEL)),
            _layer_spec(layer, (F32_SUBLANES, D_CONV)),
            _layer_spec(layer, (D_CONV, D_MODEL)),
            _layer_spec(layer, (D_POOL, D_MODEL)),
            _layer_spec(layer, (1, D_MODEL)),
            _layer_spec(layer, (D_MODEL, D_MODEL)),
            _layer_spec(layer, (1, D_MODEL)),
            _layer_spec(layer, (D_MODEL, 2 * D_FF)),
            _layer_spec(layer, (D_FF, D_MODEL)),
        ],
        out_specs=pl.BlockSpec((tm, D_MODEL), row),
        out_shape=jax.ShapeDtypeStruct((t, D_MODEL), jnp.float32),
        compiler_params=pltpu.CompilerParams(dimension_semantics=("arbitrary",), vmem_limit_bytes=VMEM_LIMIT),
        name="mix_ffn",
    )(x2, a2, cp, cp, g, wg, wao, cw, wco, wpool, ps, wo, g2, wi, wfo)


def _placement_matrices():
    pq = np.zeros((N_PIECES * LANES, LANES), np.float32)
    pk = np.zeros((N_PIECES * LANES, LANES), np.float32)
    ones_row = N_HEADS
    for h in range(N_HEADS):
        for j in range(N_PIECES):
            pq[j * LANES + h, N_PIECES * h + j] = 1.0
            pk[ones_row, N_PIECES * h + j] = 1.0
            pq[ones_row, X_ONES + N_PIECES * h + j] = 1.0
            pk[j * LANES + h, X_ONES + N_PIECES * h + j] = -1.0
    return jnp.asarray(np.concatenate([pq, pk], axis=1), jnp.bfloat16)


def kernel(x, norm_mix_g, w_in, forget_b, q_norm_g, k_norm_g, w_attn_out, conv_w, w_conv_out, pool_w, pool_scale,
           w_o, norm_ffn_g, w_ffn_in, w_ffn_out):
    bsz, seq, _ = x.shape
    depth = w_in.shape[0]
    assert seq % ROW_TILE == 0 and seq % PROJ_TILE == 0 and PROJ_TILE % K_TILE == 0 and Q_TILE == K_TILE

    bf = jnp.bfloat16
    eh = jnp.asarray(np.kron(np.eye(N_HEADS), np.ones((HEAD_DIM, HEAD_DIM))), bf)
    tri = jnp.asarray(np.tril(np.ones((K_TILE, K_TILE))), bf)
    pqk = _placement_matrices()

    row3 = lambda p: p[:, None, :]
    w_qkv = w_in[:, :, :IN_F].astype(bf)
    w_f = jnp.pad(w_in[:, :, IN_F:IN_C], ((0, 0), (0, 0), (0, LANES - N_HEADS))).astype(bf)
    w_c = w_in[:, :, IN_C:IN_G].astype(bf)
    w_g = w_in[:, :, IN_G:].astype(bf)
    fb = row3(jnp.pad(forget_b, ((0, 0), (0, LANES - N_HEADS))))
    q_scale = (HEAD_DIM ** -0.5) * LOG2E
    gq = row3(jnp.tile(q_norm_g, (1, N_HEADS))) * q_scale
    gk = row3(jnp.tile(k_norm_g, (1, N_HEADS)))
    g_mix, g_ffn, p_scale = row3(norm_mix_g), row3(norm_ffn_g), row3(pool_scale)
    wao = w_attn_out.astype(bf)
    cw = jnp.pad(conv_w, ((0, 0), (0, F32_SUBLANES - CONV_K), (0, 0)))
    wco = w_conv_out.astype(bf)
    grp_mask = jnp.asarray(np.kron(np.eye(len(POOL_WINDOWS)), np.ones((POOL_GROUP_DIM, POOL_OUT_DIM))), jnp.float32)
    wpool = (jnp.tile(pool_w.reshape(depth, D_POOL, POOL_OUT_DIM), (1, 1, len(POOL_WINDOWS))) * grp_mask).astype(bf)
    wo = w_o.astype(bf)
    wfi = w_ffn_in.astype(bf)
    wfo = w_ffn_out.astype(bf)

    x2 = x.reshape(bsz * seq, D_MODEL)
    for l in range(depth):
        qt, kh, vt, cp = _proj_call(l, x2, g_mix, w_qkv, w_f, w_c, fb, gq, gk, eh, tri, pqk, seq=seq)
        logit_bound = HEAD_DIM * jnp.max(jnp.abs(gq[l])) * jnp.max(jnp.abs(gk[l]))
        a = lax.cond(logit_bound < UNSHIFTED_LOGIT_LIMIT,
                     functools.partial(_attn_call, online=False), functools.partial(_attn_call, online=True),
                     qt, kh.reshape(bsz, seq, N_HEADS * LANES), vt)
        x2 = _mix_ffn_call(l, x2, a.reshape(bsz * seq, D_ATTN), cp, g_mix, w_g, wao, cw, wco, wpool, p_scale, wo,
                           g_ffn, wfi, wfo, seq=seq)
    return x2.reshape(bsz, seq, D_MODEL)
```

```python
import functools
import math

import numpy as np
import jax
import jax.numpy as jnp
from jax import lax
from jax.experimental import pallas as pl
from jax.experimental.pallas import tpu as pltpu

D_MODEL = 1024
HEAD_DIM = 64
D_ATTN = D_MODEL // 2
N_HEADS = D_ATTN // HEAD_DIM
D_CONV = D_MODEL // 4
CONV_K = 3
D_POOL = D_MODEL // 4
POOL_WINDOWS = (2, 4, 8, 16)
POOL_GROUP_DIM = D_POOL // len(POOL_WINDOWS)
POOL_OUT_DIM = D_MODEL // len(POOL_WINDOWS)
D_FF = -(-8 * D_MODEL // (3 * 256)) * 256
EPS = 1e-6

LANES = 128
F32_SUBLANES = 8
BF16_SUBLANES = 16
VT_ROWS = HEAD_DIM + BF16_SUBLANES
HALO = BF16_SUBLANES
assert HALO >= max(POOL_WINDOWS) - 1 and HALO >= CONV_K - 1
N_PIECES = 3
LOG2E = 1.4426950408889634
NEG_BIG = -1e30
UNSHIFTED_LOGIT_LIMIT = 100.0

ROW_TILE = 512
PROJ_TILE = 1024
SUB_ROWS = 256
Q_TILE = 512
K_TILE = 512
FF_CHUNKS = ((0, 1024), (1024, 2048), (2048, D_FF))
VMEM_LIMIT = 56 * 1024 * 1024

IN_F = 3 * D_ATTN
IN_C = IN_F + N_HEADS
IN_G = IN_C + 3 * D_CONV + D_POOL
C_CX, C_CB, C_CC, C_PX = 0, D_CONV, 2 * D_CONV, 3 * D_CONV

X_ONES = N_HEADS * N_PIECES


def _bf16(x):
    return x.astype(jnp.bfloat16)


def _dot(a, b):
    return jnp.dot(a, b, preferred_element_type=jnp.float32)


def _split3(x):
    hi = _bf16(x)
    r1 = x - hi.astype(jnp.float32)
    mid = _bf16(r1)
    lo = _bf16(r1 - mid.astype(jnp.float32))
    return hi, mid, lo


def _rms_scale(x):
    return lax.rsqrt(jnp.mean(x * x, axis=-1, keepdims=True) + EPS)


def _proj_kernel(x_ref, g_ref, wqkv_ref, wf_ref, wc_ref, fb_ref, gq_ref, gk_ref, eh_ref, tri_ref, pqk_ref,
                 qt_ref, kh_ref, vt_ref, cp_ref, carry_ref, *, tiles_per_seq):
    i = pl.program_id(0)

    @pl.when(i % tiles_per_seq == 0)
    def _():
        carry_ref[...] = jnp.zeros_like(carry_ref)

    sub = K_TILE
    lane = lax.broadcasted_iota(jnp.int32, (sub, LANES), 1)
    ex_keep = ((lane >= HEAD_DIM) & (lane < HEAD_DIM + N_PIECES)) \
        | ((lane >= HEAD_DIM + X_ONES) & (lane < HEAD_DIM + X_ONES + N_PIECES))
    ones_rows = _bf16(jnp.where(lax.broadcasted_iota(jnp.int32, (VT_ROWS - HEAD_DIM, sub), 0) == 0, 1.0, 0.0))
    carry = carry_ref[...]

    for j in range(x_ref.shape[0] // sub):
        rows = slice(j * sub, (j + 1) * sub)
        x = x_ref[rows, :]
        h = _bf16(x * _rms_scale(x) * g_ref[...])

        f = _dot(h, wf_ref[...]) + fb_ref[...]
        logf = jnp.minimum(f, 0.0) - jnp.log1p(jnp.exp(-jnp.abs(f)))
        logf = jnp.where(lane < N_HEADS, logf, 0.0)
        pieces = jnp.concatenate(_split3(logf)[:2], axis=1)
        cum = _dot(tri_ref[...], pieces)
        c = cum[:, 0:LANES] + cum[:, LANES:2 * LANES] + carry
        carry = c[sub - 1:, :]

        ap = jnp.concatenate(_split3(jnp.where(lane == N_HEADS, 1.0, c * LOG2E)), axis=1)
        ex = _dot(ap, pqk_ref[...])
        exq, exk = ex[:, 0:LANES], ex[:, LANES:2 * LANES]

        q = _dot(h, wqkv_ref[:, 0:D_ATTN])
        k = _dot(h, wqkv_ref[:, D_ATTN:2 * D_ATTN])
        ssk = _dot(_bf16(k * k), eh_ref[...])
        g_rows = jnp.tile(gq_ref[...], (1, sub // LANES))
        kn = k * lax.rsqrt(ssk * (1.0 / HEAD_DIM) + EPS) * gk_ref[...]

        for hd in range(N_HEADS):
            sl = slice((hd // 2) * LANES, (hd // 2 + 1) * LANES)
            blk_q, blk_k = q[:, sl], kn[:, sl]
            if hd % 2:
                blk_q = pltpu.roll(blk_q, HEAD_DIM, axis=1)
                blk_k = pltpu.roll(blk_k, HEAD_DIM, axis=1)
            shift = HEAD_DIM - N_PIECES * hd
            qh = jnp.where(lane < HEAD_DIM, blk_q, jnp.where(ex_keep, pltpu.roll(exq, shift, axis=1), 0.0))
            kh = jnp.where(lane < HEAD_DIM, blk_k, jnp.where(ex_keep, pltpu.roll(exk, shift, axis=1), 0.0))
            q_t = qh.T
            q_top = q_t[0:HEAD_DIM]
            r_q = lax.rsqrt(jnp.sum(q_top * q_top, axis=0, keepdims=True) * (1.0 / HEAD_DIM) + EPS)
            qt_ref[0, j, hd * LANES:(hd + 1) * LANES, :] = _bf16(
                jnp.concatenate([q_top * r_q * g_rows, q_t[HEAD_DIM:]], axis=0))
            kh_ref[rows, hd * LANES:(hd + 1) * LANES] = _bf16(kh)

        v = _dot(h, wqkv_ref[:, 2 * D_ATTN:3 * D_ATTN])
        for pr in range(N_HEADS // 2):
            vt = _bf16(v[:, pr * LANES:(pr + 1) * LANES].T)
            for hh in range(2):
                base = (2 * pr + hh) * VT_ROWS
                vt_ref[0, j, base:base + HEAD_DIM, :] = vt[hh * HEAD_DIM:(hh + 1) * HEAD_DIM, :]
                vt_ref[0, j, base + HEAD_DIM:base + VT_ROWS, :] = ones_rows

        r = _dot(h, wc_ref[...])
        cp_ref[rows, 0:D_CONV] = _bf16(r[:, C_CC:C_CC + D_CONV] * r[:, C_CX:C_CX + D_CONV])
        cp_ref[rows, D_CONV:2 * D_CONV] = _bf16(r[:, C_CB:C_CB + D_CONV])
        cp_ref[rows, 2 * D_CONV:2 * D_CONV + D_POOL] = _bf16(r[:, C_PX:C_PX + D_POOL])

    carry_ref[...] = carry


def _const_spec(shape):
    return pl.BlockSpec(shape, lambda *_: (0,) * len(shape), pipeline_mode=pl.Buffered(1))


def _layer_spec(layer, shape):
    return pl.BlockSpec((None,) + shape, lambda *_: (layer,) + (0,) * len(shape), pipeline_mode=pl.Buffered(1))


def _proj_call(layer, x2, g, wqkv, wf, wc, fb, gq, gk, eh, tri, pqk, *, seq):
    t = x2.shape[0]
    tm, kt = PROJ_TILE, K_TILE
    row = lambda i: (i, 0)
    outs = [
        jax.ShapeDtypeStruct((t // seq, seq // kt, N_HEADS * LANES, kt), jnp.bfloat16),
        jax.ShapeDtypeStruct((t, N_HEADS * LANES), jnp.bfloat16),
        jax.ShapeDtypeStruct((t // seq, seq // kt, N_HEADS * VT_ROWS, kt), jnp.bfloat16),
        jax.ShapeDtypeStruct((t, 2 * D_CONV + D_POOL), jnp.bfloat16),
    ]
    tile_t = lambda i: (i // (seq // tm), i % (seq // tm), 0, 0)
    return pl.pallas_call(
        functools.partial(_proj_kernel, tiles_per_seq=seq // tm),
        grid=(t // tm,),
        in_specs=[
            pl.BlockSpec((tm, D_MODEL), row),
            _layer_spec(layer, (1, D_MODEL)),
            _layer_spec(layer, (D_MODEL, 3 * D_ATTN)),
            _layer_spec(layer, (D_MODEL, LANES)),
            _layer_spec(layer, (D_MODEL, 3 * D_CONV + D_POOL)),
            _layer_spec(layer, (1, LANES)),
            _layer_spec(layer, (HEAD_DIM, LANES)),
            _layer_spec(layer, (1, D_ATTN)),
            _const_spec((D_ATTN, D_ATTN)),
            _const_spec((kt, kt)),
            _const_spec((N_PIECES * LANES, 2 * LANES)),
        ],
        out_specs=[
            pl.BlockSpec((1, tm // kt, N_HEADS * LANES, kt), tile_t),
            pl.BlockSpec((tm, N_HEADS * LANES), row),
            pl.BlockSpec((1, tm // kt, N_HEADS * VT_ROWS, kt), tile_t),
            pl.BlockSpec((tm, 2 * D_CONV + D_POOL), row),
        ],
        out_shape=outs,
        scratch_shapes=[pltpu.VMEM((1, LANES), jnp.float32)],
        compiler_params=pltpu.CompilerParams(dimension_semantics=("arbitrary",), vmem_limit_bytes=VMEM_LIMIT),
        name="proj",
    )(x2, g, wqkv, wf, wc, fb, gq, gk, eh, tri, pqk)


def _attn_kernel(qt_ref, kh_ref, vt_ref, o_ref, acc_sc, aux_sc, *, online):
    i = pl.program_id(1)
    tq, tk = Q_TILE, K_TILE
    heads = range(N_HEADS)

    acc_sc[...] = jnp.zeros_like(acc_sc)
    m_sc = p_sc = aux_sc
    if online:
        m_sc[...] = jnp.full_like(m_sc, NEG_BIG)

    def logits(kt, hh, masked, k0=0, k1=tk, q0=0):
        start = pl.multiple_of(kt * tk + k0, math.gcd(tk, k0))
        s = _dot(kh_ref[0, pl.ds(start, k1 - k0), hh * LANES:(hh + 1) * LANES],
                 qt_ref[0, 0, hh * LANES:(hh + 1) * LANES, q0:])
        if masked:
            k_id = lax.broadcasted_iota(jnp.int32, s.shape, 0) + (kt - i) * tk + k0
            q_id = lax.broadcasted_iota(jnp.int32, s.shape, 1) + q0
            s = jnp.where(k_id <= q_id, s, NEG_BIG)
        return s

    def vt_tile(kt, hh):
        return vt_ref[0, kt, hh * VT_ROWS:(hh + 1) * VT_ROWS, :]

    if online:
        def step(kt, masked):
            for hh in heads:
                s = logits(kt, hh, masked)
                m_prev = m_sc[hh]
                m_new = jnp.maximum(m_prev, jnp.max(s, axis=0, keepdims=True))
                m_sc[hh] = m_new
                acc_sc[hh] = jnp.exp2(m_prev - m_new) * acc_sc[hh] + _dot(vt_tile(kt, hh), _bf16(jnp.exp2(s - m_new)))

        def body(kt, carry):
            step(kt, False)
            return carry

        lax.fori_loop(0, i, body, 0)
        step(i, True)
    else:
        def stage(kt, rd, wr):
            for hh in heads:
                p_sc[wr, hh] = _bf16(jnp.exp2(logits(kt + 1, hh, False)))
                acc_sc[hh] += _dot(vt_tile(kt, hh), p_sc[rd, hh])

        def first_tile(masked):
            for hh in heads:
                p0 = _bf16(jnp.exp2(logits(0, hh, masked)))
                p_sc[0, hh] = p0
                p_sc[1, hh] = p0

        pl.when(i == 0)(functools.partial(first_tile, True))
        pl.when(i > 0)(functools.partial(first_tile, False))

        n_plain = jnp.maximum(i - 1, 0)

        @pl.when(n_plain % 2 == 1)
        def _():
            stage(0, 0, 1)

        def body(j, carry):
            kt = n_plain % 2 + 2 * j
            stage(kt, 1, 0)
            stage(kt + 1, 0, 1)
            return carry

        lax.fori_loop(0, n_plain // 2, body, 0)

        half = tk // 2

        @pl.when(i > 0)
        def _():
            for hh in heads:
                p_sc[0, hh, 0:half, :] = _bf16(jnp.exp2(logits(i, hh, True, 0, half)))
                p_sc[0, hh, half:tk, half:tq] = _bf16(jnp.exp2(logits(i, hh, True, half, tk, half)))
                acc_sc[hh] += _dot(vt_tile(i - 1, hh), p_sc[1, hh])

        for hh in heads:
            vt = vt_tile(i, hh)
            early = _dot(vt[:, 0:half], p_sc[0, hh, 0:half, :])
            late = _dot(vt[:, half:tk], p_sc[0, hh, half:tk, half:tq])
            acc_sc[hh, :, 0:half] += early[:, 0:half]
            acc_sc[hh, :, half:tq] += early[:, half:tq] + late

    for pr in range(N_HEADS // 2):
        o_t = []
        for hh in (2 * pr, 2 * pr + 1):
            acc = acc_sc[hh]
            o_t.append(acc[0:HEAD_DIM, :] / acc[HEAD_DIM:HEAD_DIM + 1, :])
        o_ref[0, :, pr * LANES:(pr + 1) * LANES] = _bf16(jnp.concatenate(o_t, axis=0).T)


def _attn_call(qt, kh, vt, *, online):
    b, s, _ = kh.shape
    tq = Q_TILE
    scratch = [pltpu.VMEM((N_HEADS, VT_ROWS, tq), jnp.float32),
               pltpu.VMEM((N_HEADS, 1, tq), jnp.float32) if online else pltpu.VMEM((2, N_HEADS, K_TILE, tq), jnp.bfloat16)]
    return pl.pallas_call(
        functools.partial(_attn_kernel, online=online),
        grid=(b, s // tq),
        in_specs=[
            pl.BlockSpec((1, 1, N_HEADS * LANES, tq), lambda bi, i: (bi, i, 0, 0)),
            pl.BlockSpec((1, s, N_HEADS * LANES), lambda bi, i: (bi, 0, 0)),
            pl.BlockSpec((1, s // K_TILE, N_HEADS * VT_ROWS, K_TILE), lambda bi, i: (bi, 0, 0, 0)),
        ],
        out_specs=pl.BlockSpec((1, tq, D_ATTN), lambda bi, i: (bi, i, 0)),
        out_shape=jax.ShapeDtypeStruct((b, s, D_ATTN), jnp.bfloat16),
        scratch_shapes=scratch,
        compiler_params=pltpu.CompilerParams(
            dimension_semantics=("arbitrary", "arbitrary"), vmem_limit_bytes=VMEM_LIMIT),
        name="attn_online" if online else "attn",
    )(qt, kh, vt)


def _shift_rows(x, k):
    return pltpu.roll(x, k, axis=0)


def _branch_inputs(cp_ref, halo_ref, cw_ref, seq_tile):
    tm = cp_ref.shape[0]
    halo = halo_ref[...].astype(jnp.float32) * jnp.where(seq_tile == 0, 0.0, 1.0)
    cp = cp_ref[...].astype(jnp.float32)
    z = jnp.concatenate([halo[:, 0:D_CONV], cp[:, 0:D_CONV]], axis=0)
    px = jnp.concatenate([halo[:, 2 * D_CONV:], cp[:, 2 * D_CONV:]], axis=0)
    cb = cp[:, D_CONV:2 * D_CONV]

    cw = cw_ref[...]
    conv = cw[2:3, :] * z + cw[1:2, :] * _shift_rows(z, 1) + cw[0:1, :] * _shift_rows(z, 2)
    u = _bf16(cb * conv[HALO:, :])

    s2 = px + _shift_rows(px, 1)
    s4 = s2 + _shift_rows(s2, 2)
    s8 = s4 + _shift_rows(s4, 4)
    s16 = s8 + _shift_rows(s8, 8)
    grp = lax.broadcasted_iota(jnp.int32, (tm, D_POOL), 1) // POOL_GROUP_DIM
    wsum = jnp.where(grp == 0, s2[HALO:], jnp.where(grp == 1, s4[HALO:], jnp.where(grp == 2, s8[HALO:], s16[HALO:])))
    pos = seq_tile * tm + lax.broadcasted_iota(jnp.int32, (tm, D_POOL), 0)
    win = jnp.left_shift(2, grp)
    counts = jnp.minimum(pos + 1, win).astype(jnp.float32)
    d = _bf16(wsum / counts - px[HALO:])
    return u, d


def _merge_rows(x, a, u, d, g_ref, wg_ref, wao_ref, wco_ref, wpool_ref, ps_ref, wo_ref):
    h = _bf16(x * _rms_scale(x) * g_ref[...])
    y_attn = _dot(a, wao_ref[...])
    y_conv = _dot(u, wco_ref[...])
    y_pool = _dot(d, wpool_ref[...]) * ps_ref[...]
    merged = jax.nn.sigmoid(_dot(h, wg_ref[:, 0:D_MODEL])) * y_attn
    merged += jax.nn.sigmoid(_dot(h, wg_ref[:, D_MODEL:2 * D_MODEL])) * y_conv
    merged += jax.nn.sigmoid(_dot(h, wg_ref[:, 2 * D_MODEL:3 * D_MODEL])) * y_pool
    return x + _dot(_bf16(merged), wo_ref[...])


def _ffn_rows(x, g_ref, wi_ref, wo_ref):
    h = _bf16(x * _rms_scale(x) * g_ref[...])
    acc = x
    for c0, c1 in FF_CHUNKS:
        gt = _dot(h, wi_ref[:, c0:c1])
        up = _dot(h, wi_ref[:, D_FF + c0:D_FF + c1])
        act = _bf16(gt * jax.nn.sigmoid(gt) * up)
        acc = acc + _dot(act, wo_ref[c0:c1, :])
    return acc


def _mix_ffn_kernel(x_ref, a_ref, cp_ref, halo_ref, g_ref, wg_ref, wao_ref, cw_ref, wco_ref, wpool_ref, ps_ref, wo_ref,
                    g2_ref, wi_ref, wfo_ref, o_ref, *, tiles_per_seq):
    seq_tile = pl.program_id(0) % tiles_per_seq
    u, d = _branch_inputs(cp_ref, halo_ref, cw_ref, seq_tile)
    blocks = [slice(r0, r0 + SUB_ROWS) for r0 in range(0, x_ref.shape[0], SUB_ROWS)]
    x1 = [_merge_rows(x_ref[rows, :], a_ref[rows, :], u[rows], d[rows], g_ref, wg_ref, wao_ref, wco_ref, wpool_ref,
                      ps_ref, wo_ref) for rows in blocks]
    for rows, x1_rows in zip(blocks, x1):
        o_ref[rows, :] = _ffn_rows(x1_rows, g2_ref, wi_ref, wfo_ref)


def _mix_ffn_call(layer, x2, a2, cp, g, wg, wao, cw, wco, wpool, ps, wo, g2, wi, wfo, *, seq):
    t = x2.shape[0]
    tm = ROW_TILE
    row = lambda i: (i, 0)
    cpw = 2 * D_CONV + D_POOL
    halo_blocks = tm // HALO
    return pl.pallas_call(
        functools.partial(_mix_ffn_kernel, tiles_per_seq=seq // tm),
        grid=(t // tm,),
        in_specs=[
            pl.BlockSpec((tm, D_MODEL), row),
            pl.BlockSpec((tm, D_ATTN), row),
            pl.BlockSpec((tm, cpw), row),
            pl.BlockSpec((HALO, cpw), lambda i: (jnp.maximum(i * halo_blocks - 1, 0), 0)),
            _layer_spec(layer, (1, D_MODEL)),
            _layer_spec(layer, (D_MODEL, 3 * D_MODEL)),
            _layer_spec(layer, (D_ATTN, D_MODEL)),
            _layer_spec(layer, (F32_SUBLANES, D_CONV)),
            _layer_spec(layer, (D_CONV, D_MODEL)),
            _layer_spec(layer, (D_POOL, D_MODEL)),
            _layer_spec(layer, (1, D_MODEL)),
            _layer_spec(layer, (D_MODEL, D_MODEL)),
            _layer_spec(layer, (1, D_MODEL)),
            _layer_spec(layer, (D_MODEL, 2 * D_FF)),
            _layer_spec(layer, (D_FF, D_MODEL)),
        ],
        out_specs=pl.BlockSpec((tm, D_MODEL), row),
        out_shape=jax.ShapeDtypeStruct((t, D_MODEL), jnp.float32),
        compiler_params=pltpu.CompilerParams(dimension_semantics=("arbitrary",), vmem_limit_bytes=VMEM_LIMIT),
        name="mix_ffn",
    )(x2, a2, cp, cp, g, wg, wao, cw, wco, wpool, ps, wo, g2, wi, wfo)


def _placement_matrices():
    pq = np.zeros((N_PIECES * LANES, LANES), np.float32)
    pk = np.zeros((N_PIECES * LANES, LANES), np.float32)
    ones_row = N_HEADS
    for h in range(N_HEADS):
        for j in range(N_PIECES):
            pq[j * LANES + h, N_PIECES * h + j] = 1.0
            pk[ones_row, N_PIECES * h + j] = 1.0
            pq[ones_row, X_ONES + N_PIECES * h + j] = 1.0
            pk[j * LANES + h, X_ONES + N_PIECES * h + j] = -1.0
    return jnp.asarray(np.concatenate([pq, pk], axis=1), jnp.bfloat16)


def kernel(x, norm_mix_g, w_in, forget_b, q_norm_g, k_norm_g, w_attn_out, conv_w, w_conv_out, pool_w, pool_scale,
           w_o, norm_ffn_g, w_ffn_in, w_ffn_out):
    bsz, seq, _ = x.shape
    depth = w_in.shape[0]
    assert seq % ROW_TILE == 0 and seq % PROJ_TILE == 0 and PROJ_TILE % K_TILE == 0 and Q_TILE == K_TILE

    bf = jnp.bfloat16
    eh = jnp.asarray(np.kron(np.eye(N_HEADS), np.ones((HEAD_DIM, HEAD_DIM))), bf)
    tri = jnp.asarray(np.tril(np.ones((K_TILE, K_TILE))), bf)
    pqk = _placement_matrices()

    row3 = lambda p: p[:, None, :]
    w_qkv = w_in[:, :, :IN_F].astype(bf)
    w_f = jnp.pad(w_in[:, :, IN_F:IN_C], ((0, 0), (0, 0), (0, LANES - N_HEADS))).astype(bf)
    w_c = w_in[:, :, IN_C:IN_G].astype(bf)
    w_g = w_in[:, :, IN_G:].astype(bf)
    fb = row3(jnp.pad(forget_b, ((0, 0), (0, LANES - N_HEADS))))
    q_scale = (HEAD_DIM ** -0.5) * LOG2E
    gq = jnp.broadcast_to((q_norm_g * q_scale)[:, :, None], (depth, HEAD_DIM, LANES))
    gk = row3(jnp.tile(k_norm_g, (1, N_HEADS)))
    g_mix, g_ffn, p_scale = row3(norm_mix_g), row3(norm_ffn_g), row3(pool_scale)
    wao = w_attn_out.astype(bf)
    cw = jnp.pad(conv_w, ((0, 0), (0, F32_SUBLANES - CONV_K), (0, 0)))
    wco = w_conv_out.astype(bf)
    grp_mask = jnp.asarray(np.kron(np.eye(len(POOL_WINDOWS)), np.ones((POOL_GROUP_DIM, POOL_OUT_DIM))), jnp.float32)
    wpool = (jnp.tile(pool_w.reshape(depth, D_POOL, POOL_OUT_DIM), (1, 1, len(POOL_WINDOWS))) * grp_mask).astype(bf)
    wo = w_o.astype(bf)
    wfi = w_ffn_in.astype(bf)
    wfo = w_ffn_out.astype(bf)

    x2 = x.reshape(bsz * seq, D_MODEL)
    for l in range(depth):
        qt, kh, vt, cp = _proj_call(l, x2, g_mix, w_qkv, w_f, w_c, fb, gq, gk, eh, tri, pqk, seq=seq)
        logit_bound = HEAD_DIM * jnp.max(jnp.abs(gq[l])) * jnp.max(jnp.abs(gk[l]))
        a = lax.cond(logit_bound < UNSHIFTED_LOGIT_LIMIT,
                     functools.partial(_attn_call, online=False), functools.partial(_attn_call, online=True),
                     qt, kh.reshape(bsz, seq, N_HEADS * LANES), vt)
        x2 = _mix_ffn_call(l, x2, a.reshape(bsz * seq, D_ATTN), cp, g_mix, w_g, wao, cw, wco, wpool, p_scale, wo,
                           g_ffn, wfi, wfo, seq=seq)
    return x2.reshape(bsz, seq, D_MODEL)
```
